```python
import jax, jax.numpy as jnp
from jax import lax
import numpy as np

D_MODEL = 4096
BATCH = 4
SEQ = 2048
DEPTH = 2
DEC_BATCH = 8
DEC_SEQ = 8
PAST_LEN = 16384
PAGE_SIZE = 128

N_A_LAYERS = DEPTH // 2
N_B_LAYERS = DEPTH - N_A_LAYERS
N_DENSE = (DEPTH + 1) // 2
N_MOE = DEPTH // 2
CONV_W = 3
HEAD_DIM = 128
N_HEADS = D_MODEL // HEAD_DIM
N_KV = 4
HPG = N_HEADS // N_KV
N_BRANCH = 3
L_CMP = 32
STRIDE = 16
R_CMP = L_CMP // STRIDE
L_SLC = 64
N_SEL = 16
WINDOW = 512
D_FF = 14336
N_EXPERTS = 8
TOP_K = 2
Q_BLOCK = 32
EPS = 1e-6
NEG = -1e30
FORCE = 1e6

kernel_name = 'yoco_shortconv_nsa_moe_step'


def rmsnorm(x, g):
    xf = x.astype(jnp.float32)
    y = xf * lax.rsqrt(jnp.mean(xf * xf, axis=-1, keepdims=True) + EPS)
    return (y * g.astype(jnp.float32)).astype(x.dtype)


def short_conv_mixer(h, conv_state, w_in, conv_w, w_out):
    t = h.shape[1]
    b_gate, c_gate, v = jnp.split(jnp.einsum('btd,de->bte', h, w_in), 3, axis=-1)
    u = c_gate * v
    u_ext = jnp.concatenate([conv_state.astype(u.dtype), u], axis=1)
    conv = sum(conv_w[j] * u_ext[:, j:j + t] for j in range(CONV_W))
    y = jnp.einsum('btd,de->bte', b_gate * conv, w_out)
    return y, u_ext[:, -(CONV_W - 1):]


def swiglu(h, w_gu, w_down):
    g, u = jnp.split(jnp.einsum('btd,df->btf', h, w_gu), 2, axis=-1)
    return jnp.einsum('btf,fd->btd', jax.nn.silu(g) * u, w_down)


def moe(h, w_router, w_gu, w_down):
    logits = jnp.einsum('btd,de->bte', h, w_router).astype(jnp.float32)
    top_v, top_i = lax.top_k(logits, TOP_K)
    wts = jax.nn.softmax(top_v, axis=-1)
    comb = jnp.sum(jax.nn.one_hot(top_i, N_EXPERTS, dtype=jnp.float32) * wts[..., None], axis=-2)
    out = jnp.zeros_like(h)
    for e in range(N_EXPERTS):
        out = out + comb[..., e:e + 1].astype(h.dtype) * swiglu(h, w_gu[e], w_down[e])
    return out


def shared_kv_rows(hs, kv_norm, w_kv, k_norm):
    b, t, _ = hs.shape
    kv = jnp.einsum('btd,de->bte', rmsnorm(hs, kv_norm), w_kv).reshape(b, t, 2 * N_BRANCH, N_KV, HEAD_DIM)
    paged = jnp.stack([kv[:, :, 0], kv[:, :, 1], rmsnorm(kv[:, :, 2], k_norm[1]), kv[:, :, 3]], axis=2)
    win = jnp.stack([rmsnorm(kv[:, :, 4], k_norm[2]), kv[:, :, 5]], axis=2)
    return paged, win


def compress(x, w1, w2, pe):
    b, t = x.shape[:2]
    n_chunk = t // STRIDE
    nc = n_chunk - R_CMP + 1
    ch = x[:, :n_chunk * STRIDE].reshape(b, n_chunk, STRIDE, N_KV, HEAD_DIM)
    pre = sum(jnp.einsum('bcsgd,sde->bcge', ch[:, r:r + nc] + pe[r * STRIDE:(r + 1) * STRIDE][:, None, :],
                         w1[r * STRIDE:(r + 1) * STRIDE]) for r in range(R_CMP))
    return jnp.einsum('bcge,ef->bcgf', jax.nn.silu(pre), w2)


def block_intersections(nc, ns):
    cs = jnp.arange(nc) * STRIDE
    ss = jnp.arange(ns) * L_SLC
    return ((cs[None, :] < ss[:, None] + L_SLC) & (cs[None, :] + L_CMP > ss[:, None])).astype(jnp.float32)


def cmp_slc_context(kv_all, cmp_w1, cmp_w2, cmp_pe, k_norm):
    b, t = kv_all.shape[:2]
    k_c = rmsnorm(compress(kv_all[:, :, 0], cmp_w1[0], cmp_w2[0], cmp_pe[0]), k_norm[0])
    v_c = compress(kv_all[:, :, 1], cmp_w1[1], cmp_w2[1], cmp_pe[1])
    nc = k_c.shape[1]
    c_end = jnp.arange(nc) * STRIDE + (L_CMP - 1)
    ns = -(-t // L_SLC)
    sl = jnp.pad(kv_all[:, :, 2:4], ((0, 0), (0, ns * L_SLC - t), (0, 0), (0, 0), (0, 0)))
    sl = sl.reshape(b, ns, L_SLC, 2, N_KV, HEAD_DIM).transpose(3, 0, 4, 1, 2, 5)
    return (k_c, v_c, c_end, sl[0], sl[1], block_intersections(nc, ns))


def query_side(h, w_qg, q_norm):
    b, t, _ = h.shape
    qg = jnp.einsum('btd,de->bte', h, w_qg)
    q = rmsnorm(qg[..., :N_HEADS * HEAD_DIM].reshape(b, t, N_KV, HPG, HEAD_DIM), q_norm)
    g = jax.nn.sigmoid(qg[..., N_HEADS * HEAD_DIM:].astype(jnp.float32)).astype(h.dtype)
    return q, g.reshape(b, t, N_KV, HPG, N_BRANCH)


def nsa_block(q, g, q_pos, k_c, v_c, c_end, k_sb, v_sb, inter, k_w, v_w, w_pos):
    bsz = q.shape[0]
    scale = HEAD_DIM ** -0.5
    s = jnp.einsum('bqghd,bcgd->bqghc', q, k_c).astype(jnp.float32) * scale
    mc = (c_end[None, :] <= q_pos[:, None])[None, :, None, None, :]
    p_c = jnp.where(mc, jax.nn.softmax(jnp.where(mc, s, NEG), axis=-1), 0.0)
    o_c = jnp.einsum('bqghc,bcgd->bqghd', p_c.astype(v_c.dtype), v_c)
    imp = jnp.einsum('bqgc,sc->bqgs', p_c.sum(3), inter)
    ns = inter.shape[0]
    j = jnp.arange(ns)[None, :]
    qblk = (q_pos // L_SLC)[:, None]
    forced = (j == 0) | (j == qblk) | (j == qblk - 1)
    score = jnp.where(forced[None, :, None, :], FORCE, imp)
    score = jnp.where((j <= qblk)[None, :, None, :], score, -FORCE)
    top_s, idx = lax.top_k(score, min(N_SEL, ns))
    sel_ok = top_s > -0.5 * FORCE
    bi = jnp.arange(bsz)[:, None, None, None]
    gi = jnp.arange(N_KV)[None, None, :, None]
    ks = k_sb[bi, gi, idx]
    vs = v_sb[bi, gi, idx]
    kpos = idx[..., None] * L_SLC + jnp.arange(L_SLC)
    ms = (sel_ok[..., None] & (kpos <= q_pos[None, :, None, None, None]))[:, :, :, None]
    s = jnp.einsum('bqghd,bqgkld->bqghkl', q, ks).astype(jnp.float32) * scale
    s = jnp.where(ms, s, NEG)
    p_s = jax.nn.softmax(s.reshape(s.shape[:4] + (-1,)), axis=-1).reshape(s.shape)
    o_s = jnp.einsum('bqghkl,bqgkld->bqghd', p_s.astype(vs.dtype), vs)
    s = jnp.einsum('bqghd,bkgd->bqghk', q, k_w).astype(jnp.float32) * scale
    dlt = q_pos[:, None] - w_pos[None, :]
    mw = ((dlt >= 0) & (dlt <= WINDOW) & (w_pos[None, :] >= 0))[None, :, None, None, :]
    p_w = jax.nn.softmax(jnp.where(mw, s, NEG), axis=-1)
    o_w = jnp.einsum('bqghk,bkgd->bqghd', p_w.astype(v_w.dtype), v_w)
    return g[..., 0:1] * o_c + g[..., 1:2] * o_s + g[..., 2:3] * o_w


def nsa_prompt(q, g, ctx, win_rows):
    b, t = q.shape[:2]
    kw_pad = jnp.pad(win_rows, ((0, 0), (WINDOW, 0), (0, 0), (0, 0), (0, 0)))

    def body(i):
        s0 = i * Q_BLOCK
        band = lax.dynamic_slice_in_dim(kw_pad, s0, WINDOW + Q_BLOCK, axis=1)
        return nsa_block(lax.dynamic_slice_in_dim(q, s0, Q_BLOCK, axis=1),
                         lax.dynamic_slice_in_dim(g, s0, Q_BLOCK, axis=1),
                         s0 + jnp.arange(Q_BLOCK), *ctx, band[:, :, 0], band[:, :, 1],
                         s0 - WINDOW + jnp.arange(WINDOW + Q_BLOCK))

    o = lax.map(body, jnp.arange(t // Q_BLOCK))
    return jnp.moveaxis(o, 0, 1).reshape(b, t, N_HEADS * HEAD_DIM)


def trunk(x, conv_state, past_kv, past_win, norm_mix, norm_ffn, conv_w_in, conv_w, conv_w_out,
          ffn_w_gu, ffn_w_down, moe_router, moe_w_gu, moe_w_down, kv_norm, w_kv, k_norm,
          cmp_w1, cmp_w2, cmp_pe, w_qg, q_norm, w_o):
    b, t, _ = x.shape
    new_conv = []
    kv_rows = win_state = ctx = win_all = None
    for l in range(DEPTH):
        h = rmsnorm(x, norm_mix[l])
        if l < N_A_LAYERS:
            y, cs = short_conv_mixer(h, conv_state[l], conv_w_in[l], conv_w[l], conv_w_out[l])
            new_conv.append(cs)
        else:
            if l == N_A_LAYERS:
                kv_rows, win_rows = shared_kv_rows(x, kv_norm, w_kv, k_norm)
                if past_kv is None:
                    kv_all, win_all = kv_rows, win_rows
                    win_state = win_rows[:, -min(WINDOW, t):]
                else:
                    kv_all = jnp.concatenate([past_kv.astype(kv_rows.dtype), kv_rows], axis=1)
                    win_all = jnp.concatenate([past_win.astype(win_rows.dtype), win_rows], axis=1)
                    win_state = win_all[:, -past_win.shape[1]:]
                ctx = cmp_slc_context(kv_all, cmp_w1, cmp_w2, cmp_pe, k_norm)
            bl = l - N_A_LAYERS
            q, g = query_side(h, w_qg[bl], q_norm[bl])
            if past_kv is None:
                o = nsa_prompt(q, g, ctx, win_all)
            else:
                pos0 = past_kv.shape[1]
                l_buf = past_win.shape[1]
                o = nsa_block(q, g, pos0 + jnp.arange(t), *ctx, win_all[:, :, 0], win_all[:, :, 1],
                              pos0 - l_buf + jnp.arange(l_buf + t)).reshape(b, t, N_HEADS * HEAD_DIM)
            y = jnp.einsum('bte,ed->btd', o, w_o[bl])
        x = x + y
        h = rmsnorm(x, norm_ffn[l])
        if l % 2 == 0:
            x = x + swiglu(h, ffn_w_gu[l // 2], ffn_w_down[l // 2])
        else:
            x = x + moe(h, moe_router[l // 2], moe_w_gu[l // 2], moe_w_down[l // 2])
    return x, jnp.stack(new_conv), kv_rows, win_state


def setup_inputs(seed: int = 0) -> dict:
    key = jax.random.key(seed)
    ks = jax.random.split(key, 32)
    f32 = jnp.float32

    def nrm(k, shape, scale):
        return jax.random.normal(k, shape, f32) * scale

    n_pages = PAST_LEN // PAGE_SIZE
    n_used = DEC_BATCH * n_pages
    n_pool = n_used + max(1, n_used // 4)
    l_win = min(WINDOW, PAST_LEN)
    d_qg = N_HEADS * HEAD_DIM + N_BRANCH * N_HEADS
    page_table = jax.random.permutation(ks[5], n_pool)[:n_used].reshape(DEC_BATCH, n_pages).astype(jnp.int32)
    return {
        'x_prompt': nrm(ks[0], (BATCH, SEQ, D_MODEL), 1.0),
        'x_sample': nrm(ks[1], (DEC_BATCH, DEC_SEQ, D_MODEL), 1.0),
        'state_conv': nrm(ks[2], (N_A_LAYERS, DEC_BATCH, CONV_W - 1, D_MODEL), 1.0),
        'cache_kv': nrm(ks[3], (n_pool, PAGE_SIZE, 4, N_KV, HEAD_DIM), 1.0),
        'cache_win': nrm(ks[4], (DEC_BATCH, l_win, 2, N_KV, HEAD_DIM), 1.0),
        'page_table': page_table,
        'norm_mix': 1.0 + nrm(ks[6], (DEPTH, D_MODEL), 0.02),
        'norm_ffn': 1.0 + nrm(ks[7], (DEPTH, D_MODEL), 0.02),
        'conv_w_in': nrm(ks[8], (N_A_LAYERS, D_MODEL, 3 * D_MODEL), D_MODEL ** -0.5),
        'conv_w': nrm(ks[9], (N_A_LAYERS, CONV_W, D_MODEL), CONV_W ** -0.5),
        'conv_w_out': nrm(ks[10], (N_A_LAYERS, D_MODEL, D_MODEL), D_MODEL ** -0.5),
        'ffn_w_gu': nrm(ks[11], (N_DENSE, D_MODEL, 2 * D_FF), D_MODEL ** -0.5),
        'ffn_w_down': nrm(ks[12], (N_DENSE, D_FF, D_MODEL), D_FF ** -0.5),
        'moe_router': nrm(ks[13], (N_MOE, D_MODEL, N_EXPERTS), D_MODEL ** -0.5),
        'moe_w_gu': nrm(ks[14], (N_MOE, N_EXPERTS, D_MODEL, 2 * D_FF), D_MODEL ** -0.5),
        'moe_w_down': nrm(ks[15], (N_MOE, N_EXPERTS, D_FF, D_MODEL), D_FF ** -0.5),
        'kv_norm': 1.0 + nrm(ks[16], (D_MODEL,), 0.02),
        'w_kv': nrm(ks[17], (D_MODEL, 2 * N_BRANCH * N_KV * HEAD_DIM), D_MODEL ** -0.5),
        'k_norm': 1.0 + nrm(ks[18], (N_BRANCH, HEAD_DIM), 0.02),
        'cmp_w1': nrm(ks[19], (2, L_CMP, HEAD_DIM, HEAD_DIM), (L_CMP * HEAD_DIM) ** -0.5),
        'cmp_w2': nrm(ks[20], (2, HEAD_DIM, HEAD_DIM), HEAD_DIM ** -0.5),
        'cmp_pe': nrm(ks[21], (2, L_CMP, HEAD_DIM), 0.5),
        'w_qg': nrm(ks[22], (N_B_LAYERS, D_MODEL, d_qg), D_MODEL ** -0.5),
        'q_norm': 1.0 + nrm(ks[23], (N_B_LAYERS, HEAD_DIM), 0.02),
        'w_o': nrm(ks[24], (N_B_LAYERS, N_HEADS * HEAD_DIM, D_MODEL), (N_HEADS * HEAD_DIM) ** -0.5),
    }


def reference(x_prompt, x_sample, state_conv, cache_kv, cache_win, page_table, norm_mix, norm_ffn,
              conv_w_in, conv_w, conv_w_out, ffn_w_gu, ffn_w_down, moe_router, moe_w_gu, moe_w_down,
              kv_norm, w_kv, k_norm, cmp_w1, cmp_w2, cmp_pe, w_qg, q_norm, w_o):
    conv0 = jnp.zeros((N_A_LAYERS, x_prompt.shape[0], CONV_W - 1, D_MODEL), x_prompt.dtype)
    y_prompt, conv_prompt, kv_prompt, win_prompt = trunk(
        x_prompt, conv0, None, None, norm_mix, norm_ffn, conv_w_in, conv_w, conv_w_out,
        ffn_w_gu, ffn_w_down, moe_router, moe_w_gu, moe_w_down, kv_norm, w_kv, k_norm,
        cmp_w1, cmp_w2, cmp_pe, w_qg, q_norm, w_o)
    past_kv = cache_kv[page_table].reshape(x_sample.shape[0], -1, 4, N_KV, HEAD_DIM)
    y_sample, conv_sample, kv_sample, win_sample = trunk(
        x_sample, state_conv, past_kv, cache_win, norm_mix, norm_ffn, conv_w_in, conv_w, conv_w_out,
        ffn_w_gu, ffn_w_down, moe_router, moe_w_gu, moe_w_down, kv_norm, w_kv, k_norm,
        cmp_w1, cmp_w2, cmp_pe, w_qg, q_norm, w_o)
    return (y_prompt, y_sample, conv_prompt, kv_prompt, win_prompt, conv_sample, kv_sample, win_sample)
```

```python
import functools

import jax
import jax.numpy as jnp
from jax import lax
from jax.experimental import pallas as pl
from jax.experimental.pallas import tpu as pltpu

HEAD_DIM = 128
N_KV = 4
N_BRANCH = 3
L_CMP = 32
STRIDE = 16
R_CMP = L_CMP // STRIDE
L_SLC = 64
N_SEL = 16
WINDOW = 512
TOP_K = 2
CONV_W = 3
EPS = 1e-6
NEG = -1e30
FORCE = 1e6

LANES = 128
VMEM_LIMIT = 56 * 1024 * 1024

F32 = jnp.float32
BF16 = jnp.bfloat16


def _pick(n, candidates):
    for c in candidates:
        if c <= n and n % c == 0:
            return c
    return n


def _params(sem):
    return pltpu.CompilerParams(dimension_semantics=sem, vmem_limit_bytes=VMEM_LIMIT)


def _sigmoid(x):
    return 1.0 / (1.0 + jnp.exp(-x))


def _div_pow2(x, n):
    assert n & (n - 1) == 0
    return jnp.right_shift(x, n.bit_length() - 1)


def _rmsnorm_kernel(x_ref, g_ref, *o_refs):
    x = x_ref[...]
    y = x * lax.rsqrt(jnp.mean(x * x, axis=-1, keepdims=True) + EPS)
    for n, o_ref in enumerate(o_refs):
        o_ref[...] = (y * g_ref[n:n + 1, :]).astype(o_ref.dtype)


def _rmsnorm(x, gains):
    m, d = x.shape
    n = gains.shape[0]
    tm = _pick(m, (192, 128, 64, 32, 16))
    outs = pl.pallas_call(
        _rmsnorm_kernel,
        out_shape=[jax.ShapeDtypeStruct((m, d), BF16)] * n,
        grid=(m // tm,),
        in_specs=[pl.BlockSpec((tm, d), lambda i: (i, 0)),
                  pl.BlockSpec((n, d), lambda i: (0, 0))],
        out_specs=[pl.BlockSpec((tm, d), lambda i: (i, 0))] * n,
        compiler_params=_params(("parallel",)),
        name="rmsnorm",
    )(x, gains)
    return outs


def _mm_kernel(*refs, nk, n_w, n_extra, n_out, epilogue):
    x_ref = refs[0]
    w_refs = refs[1:1 + n_w]
    extra = refs[1 + n_w:1 + n_w + n_extra]
    out_refs = refs[1 + n_w + n_extra:1 + n_w + n_extra + n_out]
    acc_refs = refs[1 + n_w + n_extra + n_out:]
    i = pl.program_id(0)
    j = pl.program_id(1)
    k = pl.program_id(2)

    @pl.when(k == 0)
    def _():
        for a in acc_refs:
            a[...] = jnp.zeros_like(a)

    x = x_ref[...]
    for w_ref, a in zip(w_refs, acc_refs):
        a[...] += jnp.dot(x, w_ref[...].astype(BF16), preferred_element_type=F32)

    @pl.when(k == nk - 1)
    def _():
        epilogue((i, j), [a[...] for a in acc_refs], extra, out_refs)


def _matmul(x, w, *, col_offsets, n_cols, tm, tn, tk, epilogue, extras=(), extra_specs=(),
            out_shapes, out_specs, w_row_offset=0, name):
    m, kdim = x.shape
    nk = kdim // tk
    n_w = len(col_offsets)
    in_specs = [pl.BlockSpec((tm, tk), lambda i, j, k: (i, k))]
    for off in col_offsets:
        in_specs.append(pl.BlockSpec(
            (tk, tn), lambda i, j, k, off=off: (k + w_row_offset // tk, j + off // tn)))
    in_specs += list(extra_specs)
    kern = functools.partial(_mm_kernel, nk=nk, n_w=n_w, n_extra=len(extras),
                             n_out=len(out_shapes), epilogue=epilogue)
    return pl.pallas_call(
        kern,
        out_shape=out_shapes,
        grid=(m // tm, n_cols // tn, nk),
        in_specs=in_specs,
        out_specs=out_specs,
        scratch_shapes=[pltpu.VMEM((tm, tn), F32)] * n_w,
        compiler_params=_params(("parallel", "parallel", "arbitrary")),
        name=name,
    )(x, *([w] * n_w), *extras)


def _row_tile(m):
    return _pick(m, (2064, 2048, 1024, 688, 512, 256, 192, 128, 64, 32, 16))


def _ij_spec(tm, tn):
    return pl.BlockSpec((tm, tn), lambda i, j, k: (i, j))


def _epi_conv_in(ids, accs, extra, outs):
    b, c, v = accs
    outs[0][...] = b
    outs[1][...] = c * v


def _epi_residual(ids, accs, extra, outs):
    outs[0][...] = extra[0][...] + accs[0]


def _epi_swiglu(ids, accs, extra, outs):
    g, u = accs
    outs[0][...] = (g * _sigmoid(g) * u).astype(outs[0].dtype)


def _head_rmsnorm(a, gain):
    parts = []
    for h in range(a.shape[1] // HEAD_DIM):
        ah = a[:, h * HEAD_DIM:(h + 1) * HEAD_DIM]
        ms = jnp.mean(ah * ah, axis=-1, keepdims=True)
        parts.append(ah * lax.rsqrt(ms + EPS) * gain)
    return jnp.concatenate(parts, axis=1)


def _epi_kv(ids, accs, extra, outs):
    _, j = ids
    a = accs[0]
    normed = _head_rmsnorm(a, extra[0][...])
    y = jnp.where((j == 2) | (j == 4), normed, a)
    outs[0][...] = y
    outs[1][...] = y.astype(BF16)


def _epi_q(ids, accs, extra, outs):
    outs[0][...] = _head_rmsnorm(accs[0], extra[0][...]).astype(BF16)


def _epi_gate(ids, accs, extra, outs):
    outs[0][...] = _sigmoid(accs[0])


def _epi_moe_down(e, ids, accs, extra, outs):
    resid_ref, comb_ref = extra
    outs[0][...] = resid_ref[...] + comb_ref[:, e:e + 1] * accs[0]


def _conv_kernel(u_ref, up_ref, b_ref, cw_ref, s1_ref, s2_ref, z_ref, *, tm, m_prompt, seq, dec_seq,
                 n_row_tiles):
    i = pl.program_id(0)
    u = u_ref[...]
    prev = up_ref[...]
    loc = lax.broadcasted_iota(jnp.int32, (tm, 1), 0)
    r = i * tm + loc
    u1 = pltpu.roll(u, 1, 0)
    u1 = jnp.where(loc == 0, prev[7:8, :], u1)
    u2 = pltpu.roll(u, 2, 0)
    u2 = jnp.where(loc == 0, prev[6:7, :], jnp.where(loc == 1, prev[7:8, :], u2))
    t = jnp.where(r < m_prompt, r & (seq - 1), (r - m_prompt) & (dec_seq - 1))
    u1 = jnp.where(t >= 1, u1, 0.0)
    u2 = jnp.where(t >= 2, u2, 0.0)
    w0 = cw_ref[0:1, :]
    w1 = cw_ref[1:2, :]
    w2 = cw_ref[2:3, :]
    conv = w2 * u + w1 * u1 + w0 * u2
    z_ref[...] = (b_ref[...] * conv).astype(z_ref.dtype)
    ms = s1_ref.shape[0]

    @pl.when(i == n_row_tiles - 1)
    def _():
        tail = conv[tm - ms:, :] + w1 * s1_ref[...] + w0 * s2_ref[...]
        z_ref[tm - ms:, :] = (b_ref[tm - ms:, :] * tail).astype(z_ref.dtype)


def _conv_gate(u, b, cw, s1, s2, *, m_prompt, seq, dec_seq):
    m, d = u.shape
    ms = s1.shape[0]
    tm = _row_tile(m)
    tc = _pick(d, (512, 256, 128))
    n_row_tiles = m // tm
    kern = functools.partial(_conv_kernel, tm=tm, m_prompt=m_prompt, seq=seq, dec_seq=dec_seq,
                             n_row_tiles=n_row_tiles)
    return pl.pallas_call(
        kern,
        out_shape=jax.ShapeDtypeStruct((m, d), BF16),
        grid=(n_row_tiles, d // tc),
        in_specs=[pl.BlockSpec((tm, tc), lambda i, j: (i, j)),
                  pl.BlockSpec((8, tc), lambda i, j: (jnp.maximum(i * (tm // 8) - 1, 0), j)),
                  pl.BlockSpec((tm, tc), lambda i, j: (i, j)),
                  pl.BlockSpec((CONV_W, tc), lambda i, j: (0, j)),
                  pl.BlockSpec((ms, tc), lambda i, j: (0, j)),
                  pl.BlockSpec((ms, tc), lambda i, j: (0, j))],
        out_specs=pl.BlockSpec((tm, tc), lambda i, j: (i, j)),
        compiler_params=_params(("parallel", "parallel")),
        name="conv_gate",
    )(u, u, b, cw, s1, s2)


def _cmp_stage1_compute(head_refs, w1_ref, pe_ref, top_ref, bot_ref, nch):
    for sec in range(2):
        wt = w1_ref[sec, 0].astype(BF16)
        wb = w1_ref[sec, 1].astype(BF16)
        pt = pe_ref[sec, 0]
        pb = pe_ref[sec, 1]
        rows = []
        for g in range(N_KV):
            ref = head_refs[sec * N_KV + g]
            cols = [ref[pl.ds(s, nch, stride=STRIDE), :] for s in range(STRIDE)]
            rows.append(jnp.concatenate(cols, axis=1))
        a = jnp.concatenate(rows, axis=0)
        top = jnp.dot((a + pt).astype(BF16), wt, preferred_element_type=F32)
        bot = jnp.dot((a + pb).astype(BF16), wb, preferred_element_type=F32)
        for g in range(N_KV):
            hh = sec * N_KV + g
            top_ref[:, hh * HEAD_DIM:(hh + 1) * HEAD_DIM] = top[g * nch:(g + 1) * nch, :]
            bot_ref[:, hh * HEAD_DIM:(hh + 1) * HEAD_DIM] = bot[g * nch:(g + 1) * nch, :]


def _cmp1_prompt_kernel(*refs, nch):
    n_heads = 2 * N_KV
    head_refs = refs[:n_heads]
    w1_ref, pe_ref, top_ref, bot_ref = refs[n_heads:]
    _cmp_stage1_compute(head_refs, w1_ref, pe_ref, top_ref, bot_ref, nch)


def _cmp1_sample_kernel(pt_ref, cache_ref, w1_ref, pe_ref, top_ref, bot_ref, buf_ref, sem, *,
                        pages_per_step, n_pages, page_size):
    b = pl.program_id(0)
    grp = pl.program_id(1)
    n_heads = 2 * N_KV

    def copy(p, hh):
        page = pt_ref[b * n_pages + grp * pages_per_step + p]
        return pltpu.make_async_copy(
            cache_ref.at[pl.ds(pl.multiple_of(page * page_size, page_size), page_size),
                         pl.ds(hh * HEAD_DIM, HEAD_DIM)],
            buf_ref.at[hh, pl.ds(p * page_size, page_size), :],
            sem.at[0])

    for p in range(pages_per_step):
        for hh in range(n_heads):
            copy(p, hh).start()
    for p in range(pages_per_step):
        for hh in range(n_heads):
            copy(p, hh).wait()
    _cmp_stage1_compute([buf_ref.at[hh] for hh in range(n_heads)], w1_ref, pe_ref, top_ref, bot_ref,
                        pages_per_step * page_size // STRIDE)


def _cmp2_kernel(top_ref, bot_ref, w2_ref, kn_ref, o_ref, *, nch):
    pre = top_ref[...] + pltpu.roll(bot_ref[...], nch - 1, 0)
    a = (pre * _sigmoid(pre)).astype(BF16)
    for sec in range(2):
        w2 = w2_ref[sec].astype(BF16)
        for g in range(N_KV):
            hh = sec * N_KV + g
            y = jnp.dot(a[:, hh * HEAD_DIM:(hh + 1) * HEAD_DIM], w2, preferred_element_type=F32)
            if sec == 0:
                y = _head_rmsnorm(y, kn_ref[...])
            o_ref[:, hh * HEAD_DIM:(hh + 1) * HEAD_DIM] = y.astype(o_ref.dtype)


def _cmp_stage2(top, bot, w2, kn0, nch):
    rows, width = top.shape
    return pl.pallas_call(
        functools.partial(_cmp2_kernel, nch=nch),
        out_shape=jax.ShapeDtypeStruct((rows, width), BF16),
        grid=(rows // nch,),
        in_specs=[pl.BlockSpec((nch, width), lambda b: (b, 0)),
                  pl.BlockSpec((nch, width), lambda b: (b, 0)),
                  pl.BlockSpec(w2.shape, lambda b: (0, 0, 0)),
                  pl.BlockSpec(kn0.shape, lambda b: (0, 0))],
        out_specs=pl.BlockSpec((nch, width), lambda b: (b, 0)),
        compiler_params=_params(("parallel",)),
        name="cmp_stage2",
    )(top, bot, w2, kn0)


def _compress_prompt(kv_f32, w1r, per, w2, kn0, *, batch, seq):
    width = 2 * N_KV * HEAD_DIM
    nch = seq // STRIDE
    top, bot = pl.pallas_call(
        functools.partial(_cmp1_prompt_kernel, nch=nch),
        out_shape=[jax.ShapeDtypeStruct((batch * nch, width), F32)] * 2,
        grid=(batch,),
        in_specs=[pl.BlockSpec((seq, HEAD_DIM), lambda b, hh=hh: (b, hh)) for hh in range(2 * N_KV)]
        + [pl.BlockSpec(w1r.shape, lambda b: (0, 0, 0, 0)),
           pl.BlockSpec(per.shape, lambda b: (0, 0, 0, 0))],
        out_specs=[pl.BlockSpec((nch, width), lambda b: (b, 0))] * 2,
        compiler_params=_params(("parallel",)),
        name="cmp_stage1_prompt",
    )(*([kv_f32] * (2 * N_KV)), w1r, per)
    return _cmp_stage2(top, bot, w2, kn0, nch)


def _compress_sample(cache2d, pt_flat, w1r, per, w2, kn0, *, dec_batch, n_pages, page_size):
    width = 2 * N_KV * HEAD_DIM
    pages_per_step = _pick(n_pages, (16, 8, 4, 2, 1))
    n_groups = n_pages // pages_per_step
    rows_step = pages_per_step * page_size
    nch_step = rows_step // STRIDE
    nch = n_pages * page_size // STRIDE
    kern = functools.partial(_cmp1_sample_kernel, pages_per_step=pages_per_step, n_pages=n_pages,
                             page_size=page_size)
    top, bot = pl.pallas_call(
        kern,
        out_shape=[jax.ShapeDtypeStruct((dec_batch * nch, width), F32)] * 2,
        grid_spec=pltpu.PrefetchScalarGridSpec(
            num_scalar_prefetch=1,
            grid=(dec_batch, n_groups),
            in_specs=[pl.BlockSpec(memory_space=pl.ANY),
                      pl.BlockSpec(w1r.shape, lambda b, g, pt: (0, 0, 0, 0)),
                      pl.BlockSpec(per.shape, lambda b, g, pt: (0, 0, 0, 0))],
            out_specs=[pl.BlockSpec((nch_step, width), lambda b, g, pt: (b * n_groups + g, 0))] * 2,
            scratch_shapes=[pltpu.VMEM((2 * N_KV, rows_step, HEAD_DIM), F32),
                            pltpu.SemaphoreType.DMA((1,))]),
        compiler_params=_params(("arbitrary", "arbitrary")),
        name="cmp_stage1_sample",
    )(pt_flat, cache2d, w1r, per)
    return _cmp_stage2(top, bot, w2, kn0, nch)


def _split3(x):
    hi = x.astype(BF16)
    r1 = x - hi.astype(F32)
    mid = r1.astype(BF16)
    lo = (r1 - mid.astype(F32)).astype(BF16)
    return hi, mid, lo


def _dot_f32ish(x, m_bf16):
    hi, mid, lo = _split3(x)
    out = jnp.dot(lo, m_bf16, preferred_element_type=F32)
    out = out + jnp.dot(mid, m_bf16, preferred_element_type=F32)
    return out + jnp.dot(hi, m_bf16, preferred_element_type=F32)


def _inter_t(n_c, n_s_lanes):
    c0 = lax.broadcasted_iota(jnp.int32, (n_c, n_s_lanes), 0) * STRIDE
    s0 = lax.broadcasted_iota(jnp.int32, (n_c, n_s_lanes), 1) * L_SLC
    return ((c0 < s0 + L_SLC) & (c0 + L_CMP > s0)).astype(BF16)


def _rank_desc(score, ns):
    lane = lax.broadcasted_iota(jnp.int32, score.shape, 1)
    rank = jnp.zeros(score.shape, jnp.int32)
    for sp in range(ns):
        col = score[:, sp:sp + 1]
        ahead = (col > score) | ((col == score) & (lane > sp))
        rank = rank + ahead.astype(jnp.int32)
    return rank


def _selection_scores(imp, qblk, ns):
    j = lax.broadcasted_iota(jnp.int32, imp.shape, 1)
    forced = (j == 0) | (j == qblk) | (j == qblk - 1)
    score = jnp.where(forced, FORCE, imp)
    score = jnp.where(j <= qblk, score, -FORCE)
    return jnp.where(j < ns, score, -2.0 * FORCE)


def _nsa_prompt_kernel(q_ref, g_ref, kc_ref, vc_ref, ks_ref, vs_ref, kw_ref, vw_ref, o_init_ref, o_ref,
                       m_sc, l_sc, acc_sc, *, tq, tk, hpg, seq, ncp):
    del o_init_ref
    qt = pl.program_id(2)
    t0 = qt * tq
    scale = HEAD_DIM ** -0.5
    rows = hpg * tq
    qf = q_ref[...]
    q = jnp.concatenate([qf[:, h * HEAD_DIM:(h + 1) * HEAD_DIM] for h in range(hpg)], axis=0)
    qpos = t0 + lax.broadcasted_iota(jnp.int32, (tq, 1), 0)
    nt = (((1,), (1,)), ((), ()))

    s = lax.dot_general(q, kc_ref[...], nt, preferred_element_type=F32) * scale
    c_end = lax.broadcasted_iota(jnp.int32, (tq, ncp), 1) * STRIDE + (L_CMP - 1)
    mc = (c_end <= qpos)[None]
    s3 = jnp.where(mc, s.reshape(hpg, tq, ncp), NEG)
    e = jnp.where(mc, jnp.exp(s3 - jnp.max(s3, axis=-1, keepdims=True)), 0.0)
    den = jnp.sum(e, axis=-1, keepdims=True)
    p_c = jnp.where(den > 0.0, e / jnp.where(den > 0.0, den, 1.0), 0.0)
    o_c = jnp.dot(p_c.reshape(rows, ncp).astype(BF16), vc_ref[...], preferred_element_type=F32)
    psum = jnp.sum(p_c, axis=0)

    ns = -(-seq // L_SLC)
    imp = _dot_f32ish(psum, _inter_t(ncp, LANES))
    qblk = _div_pow2(qpos, L_SLC)
    score = _selection_scores(imp, qblk, ns)
    rank = _rank_desc(score, ns)
    sel = ((rank < N_SEL) & (score > -0.5 * FORCE)).astype(BF16)

    m_sc[...] = jnp.full(m_sc.shape, NEG, F32)
    l_sc[...] = jnp.zeros(l_sc.shape, F32)
    acc_sc[...] = jnp.zeros(acc_sc.shape, F32)

    def body(j, carry):
        k0 = pl.multiple_of(j * tk, tk)
        kt = ks_ref[pl.ds(k0, tk), :]
        vt = vs_ref[pl.ds(k0, tk), :]
        sj = lax.dot_general(q, kt, nt, preferred_element_type=F32) * scale
        kpos = k0 + lax.broadcasted_iota(jnp.int32, (1, tk), 1)
        expand = (lax.broadcasted_iota(jnp.int32, (LANES, tk), 0) == _div_pow2(kpos, L_SLC)).astype(BF16)
        km = jnp.dot(sel, expand, preferred_element_type=F32) > 0.5
        mask = (km & (kpos <= qpos))[None]
        sj3 = jnp.where(mask, sj.reshape(hpg, tq, tk), NEG)
        m_old = m_sc[...]
        m_new = jnp.maximum(m_old, jnp.max(sj3, axis=-1, keepdims=True))
        p = jnp.where(mask, jnp.exp(sj3 - m_new), 0.0)
        alpha = jnp.exp(m_old - m_new)
        l_sc[...] = alpha * l_sc[...] + jnp.sum(p, axis=-1, keepdims=True)
        pv = jnp.dot(p.reshape(rows, tk).astype(BF16), vt, preferred_element_type=F32)
        acc_sc[...] = alpha * acc_sc[...] + pv.reshape(hpg, tq, HEAD_DIM)
        m_sc[...] = m_new
        return carry

    lax.fori_loop(0, (t0 + tq - 1) // tk + 1, body, 0)
    o_s = acc_sc[...] / l_sc[...]

    wl = WINDOW + tq
    w0 = pl.multiple_of(jnp.clip(t0 - WINDOW, 0, seq - wl), tq)
    kwt = kw_ref[pl.ds(w0, wl), :]
    vwt = vw_ref[pl.ds(w0, wl), :]
    sw = lax.dot_general(q, kwt, nt, preferred_element_type=F32) * scale
    wpos = w0 + lax.broadcasted_iota(jnp.int32, (1, wl), 1)
    dlt = qpos - wpos
    mw = ((dlt >= 0) & (dlt <= WINDOW))[None]
    sw3 = jnp.where(mw, sw.reshape(hpg, tq, wl), NEG)
    ew = jnp.where(mw, jnp.exp(sw3 - jnp.max(sw3, axis=-1, keepdims=True)), 0.0)
    lw = jnp.sum(ew, axis=-1, keepdims=True)
    o_w = jnp.dot(ew.reshape(rows, wl).astype(BF16), vwt, preferred_element_type=F32)
    o_w = o_w.reshape(hpg, tq, HEAD_DIM) / lw

    gt = g_ref[...]
    o_c3 = o_c.reshape(hpg, tq, HEAD_DIM)
    for h in range(hpg):
        g0 = gt[:, 0 * hpg + h:0 * hpg + h + 1]
        g1 = gt[:, 1 * hpg + h:1 * hpg + h + 1]
        g2 = gt[:, 2 * hpg + h:2 * hpg + h + 1]
        o_ref[:, h * HEAD_DIM:(h + 1) * HEAD_DIM] = (
            g0 * o_c3[h] + g1 * o_s[h] + g2 * o_w[h]).astype(o_ref.dtype)


def _nsa_prompt(q, gates, kc, kv_bf, *, m_total, batch, seq, hpg):
    d = q.shape[1]
    tq = 128
    tk = _pick(seq, (512, 256, 128))
    ncp = seq // STRIDE
    nqt = seq // tq
    gw = hpg * HEAD_DIM
    rows = hpg * tq
    kern = functools.partial(_nsa_prompt_kernel, tq=tq, tk=tk, hpg=hpg, seq=seq, ncp=ncp)

    def kvspec(sec):
        return pl.BlockSpec((seq, HEAD_DIM), lambda b, g, t, sec=sec: (b, sec * N_KV + g))

    return pl.pallas_call(
        kern,
        out_shape=jax.ShapeDtypeStruct((m_total, d), BF16),
        grid=(batch, N_KV, nqt),
        in_specs=[pl.BlockSpec((tq, gw), lambda b, g, t: (b * nqt + t, g)),
                  pl.BlockSpec((tq, LANES), lambda b, g, t: (b * nqt + t, g)),
                  pl.BlockSpec((ncp, HEAD_DIM), lambda b, g, t: (b, g)),
                  pl.BlockSpec((ncp, HEAD_DIM), lambda b, g, t: (b, N_KV + g)),
                  kvspec(2), kvspec(3), kvspec(4), kvspec(5),
                  pl.BlockSpec(memory_space=pl.ANY)],
        out_specs=pl.BlockSpec((tq, gw), lambda b, g, t: (b * nqt + t, g)),
        scratch_shapes=[pltpu.VMEM((hpg, tq, 1), F32), pltpu.VMEM((hpg, tq, 1), F32),
                        pltpu.VMEM((hpg, tq, HEAD_DIM), F32)],
        input_output_aliases={8: 0},
        compiler_params=_params(("parallel", "parallel", "arbitrary")),
        name="nsa_prompt",
    )(q, gates, kc, kc, kv_bf, kv_bf, kv_bf, kv_bf, jnp.zeros((m_total, d), BF16))


def _nsa_sample_select_kernel(q_ref, kc_ref, vc_ref, oc_ref, idx_ref, *, dec_batch, dec_seq, hpg,
                              past_len, ncp, ns, ns_lanes):
    scale = HEAD_DIM ** -0.5
    nt = (((1,), (1,)), ((), ()))
    qf = q_ref[...].astype(F32)
    rows = hpg * dec_seq
    qrow = lax.broadcasted_iota(jnp.int32, (dec_seq, 1), 0)
    qpos = past_len + qrow
    c_end = lax.broadcasted_iota(jnp.int32, (dec_seq, ncp), 1) * STRIDE + (L_CMP - 1)
    mc = (c_end <= qpos)[None]
    psums = []
    for b in range(dec_batch):
        qb = qf[b * dec_seq:(b + 1) * dec_seq, :]
        q = jnp.concatenate([qb[:, h * HEAD_DIM:(h + 1) * HEAD_DIM] for h in range(hpg)],
                            axis=0).astype(BF16)
        kc = kc_ref[b * ncp:(b + 1) * ncp, :]
        vc = vc_ref[b * ncp:(b + 1) * ncp, :]
        s = lax.dot_general(q, kc, nt, preferred_element_type=F32) * scale
        s3 = jnp.where(mc, s.reshape(hpg, dec_seq, ncp), NEG)
        e = jnp.where(mc, jnp.exp(s3 - jnp.max(s3, axis=-1, keepdims=True)), 0.0)
        den = jnp.sum(e, axis=-1, keepdims=True)
        p_c = jnp.where(den > 0.0, e / jnp.where(den > 0.0, den, 1.0), 0.0)
        o_c = jnp.dot(p_c.reshape(rows, ncp).astype(BF16), vc, preferred_element_type=F32)
        for h in range(hpg):
            oc_ref[b * dec_seq:(b + 1) * dec_seq, h * HEAD_DIM:(h + 1) * HEAD_DIM] = (
                o_c[h * dec_seq:(h + 1) * dec_seq, :])
        psums.append(jnp.sum(p_c, axis=0))
    psum = jnp.concatenate(psums, axis=0)
    imp = _dot_f32ish(psum, _inter_t(ncp, ns_lanes))
    n_rows = dec_batch * dec_seq
    qpos_all = past_len + (lax.broadcasted_iota(jnp.int32, (n_rows, 1), 0) & (dec_seq - 1))
    score = _selection_scores(imp, _div_pow2(qpos_all, L_SLC), ns)
    rank = _rank_desc(score, ns)
    ok = score > -0.5 * FORCE
    lane = lax.broadcasted_iota(jnp.int32, score.shape, 1)
    out_lane = lax.broadcasted_iota(jnp.int32, (n_rows, LANES), 1)
    idx = jnp.full((n_rows, LANES), -1, jnp.int32)
    for k in range(N_SEL):
        hit = (rank == k) & ok
        blk = jnp.sum(jnp.where(hit, (lane + 1).astype(F32), 0.0), axis=-1, keepdims=True)
        blk = blk.astype(jnp.int32) - 1
        idx = jnp.where(out_lane == k, blk, idx)
    idx_ref[...] = idx


def _nsa_sample_attend_kernel(idx_ref, pt_ref, q_ref, g_ref, oc_ref, knew_ref, vnew_ref, kwo_ref, vwo_ref,
                              kwn_ref, vwn_ref, cache_ref, o_in_ref, o_ref, kbuf, vbuf, of32, sems, *,
                              dec_batch, dec_seq, hpg, past_len, n_pages, page_size, l_win):
    del o_in_ref
    g = pl.program_id(0)
    scale = HEAD_DIM ** -0.5
    nt = (((1,), (1,)), ((), ()))
    rows = hpg * dec_seq
    n_past_blocks = past_len // L_SLC
    blocks_per_page = page_size // L_SLC
    nkeys = N_SEL * L_SLC
    qf = q_ref[...].astype(F32)
    gt = g_ref[...]
    kcol = pl.multiple_of((2 * N_KV + g) * HEAD_DIM, HEAD_DIM)
    vcol = pl.multiple_of((3 * N_KV + g) * HEAD_DIM, HEAD_DIM)
    qrow = lax.broadcasted_iota(jnp.int32, (rows, 1), 0) & (dec_seq - 1)
    key_slot = _div_pow2(lax.broadcasted_iota(jnp.int32, (1, nkeys), 1), L_SLC)
    newj = lax.broadcasted_iota(jnp.int32, (1, dec_seq), 1)

    for b in range(dec_batch):
        def copies(qi, k):
            blk = idx_ref[((g * dec_batch + b) * dec_seq + qi) * N_SEL + k]
            blk = jnp.clip(blk, 0, n_past_blocks - 1)
            page = pt_ref[b * n_pages + blk // blocks_per_page]
            row0 = pl.multiple_of(page * page_size + (blk % blocks_per_page) * L_SLC, L_SLC)
            ck = pltpu.make_async_copy(cache_ref.at[pl.ds(row0, L_SLC), pl.ds(kcol, HEAD_DIM)],
                                       kbuf.at[qi, pl.ds(k * L_SLC, L_SLC), :], sems.at[0])
            cv = pltpu.make_async_copy(cache_ref.at[pl.ds(row0, L_SLC), pl.ds(vcol, HEAD_DIM)],
                                       vbuf.at[qi, pl.ds(k * L_SLC, L_SLC), :], sems.at[1])
            return ck, cv

        for qi in range(dec_seq):
            for k in range(N_SEL):
                ck, cv = copies(qi, k)
                ck.start()
                cv.start()
        for qi in range(dec_seq):
            for k in range(N_SEL):
                ck, cv = copies(qi, k)
                ck.wait()
                cv.wait()

        qb = qf[b * dec_seq:(b + 1) * dec_seq, :]
        q = jnp.concatenate([qb[:, h * HEAD_DIM:(h + 1) * HEAD_DIM] for h in range(hpg)],
                            axis=0).astype(BF16)
        knew = knew_ref[b * dec_seq:(b + 1) * dec_seq, :].astype(BF16)
        vnew = vnew_ref[b * dec_seq:(b + 1) * dec_seq, :].astype(BF16)
        s_new = lax.dot_general(q, knew, nt, preferred_element_type=F32) * scale

        o_s = jnp.zeros((rows, HEAD_DIM), F32)
        for qi in range(dec_seq):
            valid = jnp.zeros((1, nkeys), jnp.int32)
            has_new = jnp.int32(0)
            for k in range(N_SEL):
                blk = idx_ref[((g * dec_batch + b) * dec_seq + qi) * N_SEL + k]
                is_past = ((blk >= 0) & (blk < n_past_blocks)).astype(jnp.int32)
                valid = jnp.where(key_slot == k, is_past, valid)
                has_new = has_new | (blk == n_past_blocks).astype(jnp.int32)
            mk = valid > 0
            mn = ((newj <= qi).astype(jnp.int32) * has_new) > 0
            kq = kbuf[qi].astype(BF16)
            vq = vbuf[qi].astype(BF16)
            s_old = lax.dot_general(q, kq, nt, preferred_element_type=F32) * scale
            s_old = jnp.where(mk, s_old, NEG)
            s_n = jnp.where(mn, s_new, NEG)
            mx = jnp.maximum(jnp.max(s_old, axis=-1, keepdims=True), jnp.max(s_n, axis=-1, keepdims=True))
            p_old = jnp.where(mk, jnp.exp(s_old - mx), 0.0)
            p_n = jnp.where(mn, jnp.exp(s_n - mx), 0.0)
            den = jnp.sum(p_old, axis=-1, keepdims=True) + jnp.sum(p_n, axis=-1, keepdims=True)
            o = (jnp.dot(p_old.astype(BF16), vq, preferred_element_type=F32)
                 + jnp.dot(p_n.astype(BF16), vnew, preferred_element_type=F32)) / den
            o_s = jnp.where(qrow == qi, o, o_s)

        kwo = kwo_ref[b * l_win:(b + 1) * l_win, :].astype(BF16)
        vwo = vwo_ref[b * l_win:(b + 1) * l_win, :].astype(BF16)
        kwn = kwn_ref[b * dec_seq:(b + 1) * dec_seq, :].astype(BF16)
        vwn = vwn_ref[b * dec_seq:(b + 1) * dec_seq, :].astype(BF16)
        qpos = past_len + qrow
        wpos_o = past_len - l_win + lax.broadcasted_iota(jnp.int32, (1, l_win), 1)
        wpos_n = past_len + newj
        d_o = qpos - wpos_o
        d_n = qpos - wpos_n
        m_o = (d_o >= 0) & (d_o <= WINDOW) & (wpos_o >= 0)
        m_n = (d_n >= 0) & (d_n <= WINDOW)
        sw_o = jnp.where(m_o, lax.dot_general(q, kwo, nt, preferred_element_type=F32) * scale, NEG)
        sw_n = jnp.where(m_n, lax.dot_general(q, kwn, nt, preferred_element_type=F32) * scale, NEG)
        mx = jnp.maximum(jnp.max(sw_o, axis=-1, keepdims=True), jnp.max(sw_n, axis=-1, keepdims=True))
        pw_o = jnp.where(m_o, jnp.exp(sw_o - mx), 0.0)
        pw_n = jnp.where(m_n, jnp.exp(sw_n - mx), 0.0)
        den = jnp.sum(pw_o, axis=-1, keepdims=True) + jnp.sum(pw_n, axis=-1, keepdims=True)
        o_w = (jnp.dot(pw_o.astype(BF16), vwo, preferred_element_type=F32)
               + jnp.dot(pw_n.astype(BF16), vwn, preferred_element_type=F32)) / den

        gb = gt[b * dec_seq:(b + 1) * dec_seq, :]
        for h in range(hpg):
            sl = slice(h * dec_seq, (h + 1) * dec_seq)
            g0 = gb[:, 0 * hpg + h:0 * hpg + h + 1]
            g1 = gb[:, 1 * hpg + h:1 * hpg + h + 1]
            g2 = gb[:, 2 * hpg + h:2 * hpg + h + 1]
            o_c = oc_ref[b * dec_seq:(b + 1) * dec_seq, h * HEAD_DIM:(h + 1) * HEAD_DIM]
            of32[b * dec_seq:(b + 1) * dec_seq, h * HEAD_DIM:(h + 1) * HEAD_DIM] = (
                g0 * o_c + g1 * o_s[sl, :] + g2 * o_w[sl, :])
    o_ref[...] = of32[...].astype(o_ref.dtype)


def _nsa_sample(q, gates, kc_s, kv_f32, cache2d, win2d, pt_flat, o_prompt, *, m_prompt, dec_batch,
                dec_seq, hpg, past_len, n_pages, page_size, l_win):
    ms = dec_batch * dec_seq
    gw = hpg * HEAD_DIM
    ncp = past_len // STRIDE
    t_all = past_len + dec_seq
    ns = -(-t_all // L_SLC)
    ns_lanes = -(-ns // LANES) * LANES
    rb = m_prompt // ms
    sel = functools.partial(_nsa_sample_select_kernel, dec_batch=dec_batch, dec_seq=dec_seq, hpg=hpg,
                            past_len=past_len, ncp=ncp, ns=ns, ns_lanes=ns_lanes)
    oc, idx = pl.pallas_call(
        sel,
        out_shape=[jax.ShapeDtypeStruct((ms, N_KV * gw), F32),
                   jax.ShapeDtypeStruct((N_KV * ms, LANES), jnp.int32)],
        grid=(N_KV,),
        in_specs=[pl.BlockSpec((ms, gw), lambda g: (rb, g)),
                  pl.BlockSpec((dec_batch * ncp, HEAD_DIM), lambda g: (0, g)),
                  pl.BlockSpec((dec_batch * ncp, HEAD_DIM), lambda g: (0, N_KV + g))],
        out_specs=[pl.BlockSpec((ms, gw), lambda g: (0, g)),
                   pl.BlockSpec((ms, LANES), lambda g: (g, 0))],
        compiler_params=_params(("parallel",)),
        name="nsa_sample_select",
    )(q, kc_s, kc_s)
    idx_flat = idx[:, :N_SEL].reshape(-1)

    att = functools.partial(_nsa_sample_attend_kernel, dec_batch=dec_batch, dec_seq=dec_seq, hpg=hpg,
                            past_len=past_len, n_pages=n_pages, page_size=page_size, l_win=l_win)

    def sm(f):
        return lambda g, idx_r, pt_r: f(g)

    return pl.pallas_call(
        att,
        out_shape=jax.ShapeDtypeStruct(o_prompt.shape, o_prompt.dtype),
        grid_spec=pltpu.PrefetchScalarGridSpec(
            num_scalar_prefetch=2,
            grid=(N_KV,),
            in_specs=[pl.BlockSpec((ms, gw), sm(lambda g: (rb, g))),
                      pl.BlockSpec((ms, LANES), sm(lambda g: (rb, g))),
                      pl.BlockSpec((ms, gw), sm(lambda g: (0, g))),
                      pl.BlockSpec((ms, HEAD_DIM), sm(lambda g: (rb, 2 * N_KV + g))),
                      pl.BlockSpec((ms, HEAD_DIM), sm(lambda g: (rb, 3 * N_KV + g))),
                      pl.BlockSpec((dec_batch * l_win, HEAD_DIM), sm(lambda g: (0, g))),
                      pl.BlockSpec((dec_batch * l_win, HEAD_DIM), sm(lambda g: (0, N_KV + g))),
                      pl.BlockSpec((ms, HEAD_DIM), sm(lambda g: (rb, 4 * N_KV + g))),
                      pl.BlockSpec((ms, HEAD_DIM), sm(lambda g: (rb, 5 * N_KV + g))),
                      pl.BlockSpec(memory_space=pl.ANY),
                      pl.BlockSpec(memory_space=pl.ANY)],
            out_specs=pl.BlockSpec((ms, gw), sm(lambda g: (rb, g))),
            scratch_shapes=[pltpu.VMEM((dec_seq, N_SEL * L_SLC, HEAD_DIM), F32),
                            pltpu.VMEM((dec_seq, N_SEL * L_SLC, HEAD_DIM), F32),
                            pltpu.VMEM((ms, gw), F32),
                            pltpu.SemaphoreType.DMA((2,))]),
        input_output_aliases={12: 0},
        compiler_params=_params(("arbitrary",)),
        name="nsa_sample_attend",
    )(idx_flat, pt_flat, q, gates, oc, kv_f32, kv_f32, win2d, win2d, kv_f32, kv_f32, cache2d, o_prompt)


def _router_kernel(x_ref, g_ref, r_ref, h_ref, comb_ref, *, n_experts):
    x = x_ref[...]
    h = x * lax.rsqrt(jnp.mean(x * x, axis=-1, keepdims=True) + EPS) * g_ref[...]
    h_ref[...] = h.astype(h_ref.dtype)
    r_hi, r_mid, r_lo = _split3(r_ref[...])
    h_hi, h_mid, h_lo = _split3(h)

    def d(a, b):
        return jnp.dot(a, b, preferred_element_type=F32)

    logits = (d(h_lo, r_hi) + d(h_hi, r_lo) + d(h_mid, r_mid)) + (d(h_mid, r_hi) + d(h_hi, r_mid)) + d(h_hi, r_hi)
    lane = lax.broadcasted_iota(jnp.int32, logits.shape, 1)
    logits = jnp.where(lane < n_experts, logits, -jnp.inf)
    v1 = jnp.max(logits, axis=-1, keepdims=True)
    i1 = jnp.min(jnp.where(logits == v1, lane, LANES), axis=-1, keepdims=True)
    rest = jnp.where(lane == i1, -jnp.inf, logits)
    v2 = jnp.max(rest, axis=-1, keepdims=True)
    i2 = jnp.min(jnp.where(rest == v2, lane, LANES), axis=-1, keepdims=True)
    e2 = jnp.exp(v2 - v1)
    w1 = 1.0 / (1.0 + e2)
    w2 = e2 / (1.0 + e2)
    comb_ref[...] = jnp.where(lane == i1, w1, 0.0) + jnp.where(lane == i2, w2, 0.0)


def _router(x, gain, router_p, n_experts):
    m, d = x.shape
    tm = _pick(m, (192, 128, 64, 32, 16))
    return pl.pallas_call(
        functools.partial(_router_kernel, n_experts=n_experts),
        out_shape=[jax.ShapeDtypeStruct((m, d), BF16), jax.ShapeDtypeStruct((m, LANES), F32)],
        grid=(m // tm,),
        in_specs=[pl.BlockSpec((tm, d), lambda i: (i, 0)),
                  pl.BlockSpec((1, d), lambda i: (0, 0)),
                  pl.BlockSpec((d, LANES), lambda i: (0, 0))],
        out_specs=[pl.BlockSpec((tm, d), lambda i: (i, 0)),
                   pl.BlockSpec((tm, LANES), lambda i: (i, 0))],
        compiler_params=_params(("parallel",)),
        name="moe_router",
    )(x, gain, router_p)


def kernel(x_prompt, x_sample, state_conv, cache_kv, cache_win, page_table, norm_mix, norm_ffn, conv_w_in, conv_w, conv_w_out, ffn_w_gu, ffn_w_down, moe_router, moe_w_gu, moe_w_down, kv_norm, w_kv, k_norm, cmp_w1, cmp_w2, cmp_pe, w_qg, q_norm, w_o):
    batch, seq, d = x_prompt.shape
    dec_batch, dec_seq, _ = x_sample.shape
    n_pool, page_size = cache_kv.shape[:2]
    n_pages = page_table.shape[1]
    past_len = n_pages * page_size
    l_win = cache_win.shape[1]
    d_ff = ffn_w_down.shape[1]
    n_experts = moe_router.shape[2]
    n_heads = d // HEAD_DIM
    hpg = n_heads // N_KV
    sec_w = N_KV * HEAD_DIM
    m_prompt = batch * seq
    ms = dec_batch * dec_seq
    m = m_prompt + ms
    assert seq & (seq - 1) == 0 and dec_seq & (dec_seq - 1) == 0 and dec_seq >= CONV_W - 1
    assert m_prompt % ms == 0 and ms % 16 == 0 and seq % 128 == 0 and seq >= WINDOW + 128
    assert past_len % L_SLC == 0 and dec_seq <= L_SLC and page_size % L_SLC == 0
    assert norm_mix.shape[0] == 2 and l_win == WINDOW and past_len >= l_win

    x0 = jnp.concatenate([x_prompt.reshape(m_prompt, d), x_sample.reshape(ms, d)], axis=0)
    tm = _row_tile(m)

    (h0,) = _rmsnorm(x0, norm_mix[0:1])
    tn = _pick(d, (256, 128))
    tk = _pick(d, (1024, 512, 256, 128))
    b_gate, u = _matmul(
        h0, conv_w_in[0], col_offsets=(0, d, 2 * d), n_cols=d, tm=tm, tn=tn, tk=tk,
        epilogue=_epi_conv_in, out_shapes=[jax.ShapeDtypeStruct((m, d), F32)] * 2,
        out_specs=[_ij_spec(tm, tn)] * 2, name="conv_in")
    st = state_conv[0]
    zrow = jnp.zeros((dec_batch, dec_seq - 1, d), F32)
    s1 = jnp.concatenate([st[:, 1:2], zrow], axis=1).reshape(ms, d)
    s2 = jnp.concatenate([st[:, 0:1], st[:, 1:2], zrow[:, 1:]], axis=1).reshape(ms, d)
    z = _conv_gate(u, b_gate, conv_w[0], s1, s2, m_prompt=m_prompt, seq=seq, dec_seq=dec_seq)
    tn = _pick(d, (512, 256, 128))
    (x1,) = _matmul(
        z, conv_w_out[0], col_offsets=(0,), n_cols=d, tm=tm, tn=tn, tk=tk, epilogue=_epi_residual,
        extras=(x0,), extra_specs=(_ij_spec(tm, tn),), out_shapes=[jax.ShapeDtypeStruct((m, d), F32)],
        out_specs=[_ij_spec(tm, tn)], name="conv_out")
    u_p = u[:m_prompt].reshape(batch, seq, d)
    conv_prompt = u_p[:, seq - (CONV_W - 1):][None]
    conv_sample = u[m_prompt:].reshape(dec_batch, dec_seq, d)[:, dec_seq - (CONV_W - 1):][None]

    (h1,) = _rmsnorm(x1, norm_ffn[0:1])
    tnf = _pick(d_ff, (512, 256, 128))
    (act,) = _matmul(
        h1, ffn_w_gu[0], col_offsets=(0, d_ff), n_cols=d_ff, tm=tm, tn=tnf, tk=tk, epilogue=_epi_swiglu,
        out_shapes=[jax.ShapeDtypeStruct((m, d_ff), BF16)], out_specs=[_ij_spec(tm, tnf)], name="ffn_gu")
    tkf = _pick(d_ff, (1024, 512, 256, 128))
    (x2,) = _matmul(
        act, ffn_w_down[0], col_offsets=(0,), n_cols=d, tm=tm, tn=tn, tk=tkf, epilogue=_epi_residual,
        extras=(x1,), extra_specs=(_ij_spec(tm, tn),), out_shapes=[jax.ShapeDtypeStruct((m, d), F32)],
        out_specs=[_ij_spec(tm, tn)], name="ffn_down")

    hkv, h2 = _rmsnorm(x2, jnp.stack([kv_norm, norm_mix[1]]))
    ones = jnp.ones((HEAD_DIM,), F32)
    kv_gain = jnp.stack([ones, ones, k_norm[1], ones, k_norm[2], ones]).reshape(2 * N_BRANCH, 1, HEAD_DIM)
    kv_f32, kv_bf = _matmul(
        hkv, w_kv, col_offsets=(0,), n_cols=2 * N_BRANCH * sec_w, tm=tm, tn=sec_w, tk=tk, epilogue=_epi_kv,
        extras=(kv_gain,), extra_specs=(pl.BlockSpec((None, 1, HEAD_DIM), lambda i, j, k: (j, 0, 0)),),
        out_shapes=[jax.ShapeDtypeStruct((m, 2 * N_BRANCH * sec_w), F32),
                    jax.ShapeDtypeStruct((m, 2 * N_BRANCH * sec_w), BF16)],
        out_specs=[_ij_spec(tm, sec_w)] * 2, name="kv_proj")
    kv_p = kv_f32[:m_prompt].reshape(batch, seq, 2 * N_BRANCH, N_KV, HEAD_DIM)
    kv_s = kv_f32[m_prompt:].reshape(dec_batch, dec_seq, 2 * N_BRANCH, N_KV, HEAD_DIM)
    kv_prompt = kv_p[:, :, :4]
    win_prompt = kv_p[:, seq - min(WINDOW, seq):, 4:]
    kv_sample = kv_s[:, :, :4]
    win_sample = jnp.concatenate([cache_win, kv_s[:, :, 4:]], axis=1)[:, dec_seq:]

    wq = w_qg[0]
    (q,) = _matmul(
        h2, wq, col_offsets=(0,), n_cols=d, tm=tm, tn=sec_w, tk=tk, epilogue=_epi_q,
        extras=(q_norm[0:1],), extra_specs=(pl.BlockSpec((1, HEAD_DIM), lambda i, j, k: (0, 0)),),
        out_shapes=[jax.ShapeDtypeStruct((m, d), BF16)], out_specs=[_ij_spec(tm, sec_w)], name="q_proj")
    wg = wq[:, d:].reshape(d, N_KV, hpg, N_BRANCH).transpose(0, 1, 3, 2).reshape(d, N_KV, N_BRANCH * hpg)
    wg = jnp.pad(wg, ((0, 0), (0, 0), (0, LANES - N_BRANCH * hpg))).reshape(d, N_KV * LANES)
    (gates,) = _matmul(
        h2, wg, col_offsets=(0,), n_cols=N_KV * LANES, tm=tm, tn=N_KV * LANES, tk=tk, epilogue=_epi_gate,
        out_shapes=[jax.ShapeDtypeStruct((m, N_KV * LANES), F32)], out_specs=[_ij_spec(tm, N_KV * LANES)],
        name="gate_proj")

    w1r = cmp_w1.reshape(2, R_CMP, STRIDE * HEAD_DIM, HEAD_DIM)
    per = cmp_pe.reshape(2, R_CMP, 1, STRIDE * HEAD_DIM)
    kn0 = k_norm[0:1]
    kc_p = _compress_prompt(kv_f32, w1r, per, cmp_w2, kn0, batch=batch, seq=seq)
    cache2d = cache_kv.reshape(n_pool * page_size, 4 * sec_w)
    pt_flat = page_table.reshape(-1)
    kc_s = _compress_sample(cache2d, pt_flat, w1r, per, cmp_w2, kn0, dec_batch=dec_batch, n_pages=n_pages,
                            page_size=page_size)

    o = _nsa_prompt(q, gates, kc_p, kv_bf, m_total=m, batch=batch, seq=seq, hpg=hpg)
    win2d = cache_win.reshape(dec_batch * l_win, 2 * sec_w)
    o = _nsa_sample(q, gates, kc_s, kv_f32, cache2d, win2d, pt_flat, o, m_prompt=m_prompt,
                    dec_batch=dec_batch, dec_seq=dec_seq, hpg=hpg, past_len=past_len, n_pages=n_pages,
                    page_size=page_size, l_win=l_win)
    (x3,) = _matmul(
        o, w_o[0], col_offsets=(0,), n_cols=d, tm=tm, tn=tn, tk=tk, epilogue=_epi_residual,
        extras=(x2,), extra_specs=(_ij_spec(tm, tn),), out_shapes=[jax.ShapeDtypeStruct((m, d), F32)],
        out_specs=[_ij_spec(tm, tn)], name="attn_out")

    router_p = jnp.pad(moe_router[0], ((0, 0), (0, LANES - n_experts)))
    h3, comb = _router(x3, norm_ffn[1:2], router_p, n_experts)
    wgu = moe_w_gu[0].reshape(n_experts * d, 2 * d_ff)
    wdn = moe_w_down[0].reshape(n_experts * d_ff, d)
    x4 = x3
    for e in range(n_experts):
        (act_e,) = _matmul(
            h3, wgu, col_offsets=(0, d_ff), n_cols=d_ff, tm=tm, tn=tnf, tk=tk, epilogue=_epi_swiglu,
            out_shapes=[jax.ShapeDtypeStruct((m, d_ff), BF16)], out_specs=[_ij_spec(tm, tnf)],
            w_row_offset=e * d, name="moe_gu")
        (x4,) = _matmul(
            act_e, wdn, col_offsets=(0,), n_cols=d, tm=tm, tn=tn, tk=tkf,
            epilogue=functools.partial(_epi_moe_down, e), extras=(x4, comb),
            extra_specs=(_ij_spec(tm, tn), pl.BlockSpec((tm, LANES), lambda i, j, k: (i, 0))),
            out_shapes=[jax.ShapeDtypeStruct((m, d), F32)], out_specs=[_ij_spec(tm, tn)],
            w_row_offset=e * d_ff, name="moe_down")

    y_prompt = x4[:m_prompt].reshape(batch, seq, d)
    y_sample = x4[m_prompt:].reshape(dec_batch, dec_seq, d)
    return (y_prompt, y_sample, conv_prompt, kv_prompt, win_prompt, conv_sample, kv_sample, win_sample)
```

```python
import functools

import jax
import jax.numpy as jnp
from jax import lax
from jax.experimental import pallas as pl
from jax.experimental.pallas import tpu as pltpu

HEAD_DIM = 128
N_KV = 4
N_BRANCH = 3
L_CMP = 32
STRIDE = 16
R_CMP = L_CMP // STRIDE
L_SLC = 64
N_SEL = 16
WINDOW = 512
TOP_K = 2
CONV_W = 3
EPS = 1e-6
NEG = -1e30
FORCE = 1e6

LANES = 128
MOE_SUB_ROWS = 128
VMEM_LIMIT = 56 * 1024 * 1024

F32 = jnp.float32
BF16 = jnp.bfloat16


def _pick(n, candidates):
    for c in candidates:
        if c <= n and n % c == 0:
            return c
    return n


def _params(sem):
    return pltpu.CompilerParams(dimension_semantics=sem, vmem_limit_bytes=VMEM_LIMIT)


def _sigmoid(x):
    return 1.0 / (1.0 + jnp.exp(-x))


def _div_pow2(x, n):
    assert n & (n - 1) == 0
    return jnp.right_shift(x, n.bit_length() - 1)


def _rmsnorm_kernel(x_ref, g_ref, *o_refs):
    x = x_ref[...]
    y = x * lax.rsqrt(jnp.mean(x * x, axis=-1, keepdims=True) + EPS)
    for n, o_ref in enumerate(o_refs):
        o_ref[...] = (y * g_ref[n:n + 1, :]).astype(o_ref.dtype)


def _rmsnorm(x, gains):
    m, d = x.shape
    n = gains.shape[0]
    tm = _pick(m, (192, 128, 64, 32, 16))
    outs = pl.pallas_call(
        _rmsnorm_kernel,
        out_shape=[jax.ShapeDtypeStruct((m, d), BF16)] * n,
        grid=(m // tm,),
        in_specs=[pl.BlockSpec((tm, d), lambda i: (i, 0)),
                  pl.BlockSpec((n, d), lambda i: (0, 0))],
        out_specs=[pl.BlockSpec((tm, d), lambda i: (i, 0))] * n,
        compiler_params=_params(("parallel",)),
        name="rmsnorm",
    )(x, gains)
    return outs


def _mm_kernel(*refs, nk, n_w, n_extra, n_out, epilogue):
    x_ref = refs[0]
    w_refs = refs[1:1 + n_w]
    extra = refs[1 + n_w:1 + n_w + n_extra]
    out_refs = refs[1 + n_w + n_extra:1 + n_w + n_extra + n_out]
    acc_refs = refs[1 + n_w + n_extra + n_out:]
    i = pl.program_id(0)
    j = pl.program_id(1)
    k = pl.program_id(2)

    @pl.when(k == 0)
    def _():
        for a in acc_refs:
            a[...] = jnp.zeros_like(a)

    x = x_ref[...]
    for w_ref, a in zip(w_refs, acc_refs):
        a[...] += jnp.dot(x, w_ref[...].astype(BF16), preferred_element_type=F32)

    @pl.when(k == nk - 1)
    def _():
        epilogue((i, j), [a[...] for a in acc_refs], extra, out_refs)


def _matmul(x, w, *, col_offsets, n_cols, tm, tn, tk, epilogue, extras=(), extra_specs=(),
            out_shapes, out_specs, w_row_offset=0, name):
    m, kdim = x.shape
    nk = kdim // tk
    n_w = len(col_offsets)
    in_specs = [pl.BlockSpec((tm, tk), lambda i, j, k: (i, k))]
    for off in col_offsets:
        in_specs.append(pl.BlockSpec(
            (tk, tn), lambda i, j, k, off=off: (k + w_row_offset // tk, j + off // tn)))
    in_specs += list(extra_specs)
    kern = functools.partial(_mm_kernel, nk=nk, n_w=n_w, n_extra=len(extras),
                             n_out=len(out_shapes), epilogue=epilogue)
    return pl.pallas_call(
        kern,
        out_shape=out_shapes,
        grid=(m // tm, n_cols // tn, nk),
        in_specs=in_specs,
        out_specs=out_specs,
        scratch_shapes=[pltpu.VMEM((tm, tn), F32)] * n_w,
        compiler_params=_params(("parallel", "parallel", "arbitrary")),
        name=name,
    )(x, *([w] * n_w), *extras)


def _row_tile(m):
    return _pick(m, (2064, 2048, 1024, 688, 512, 256, 192, 128, 64, 32, 16))


def _ij_spec(tm, tn):
    return pl.BlockSpec((tm, tn), lambda i, j, k: (i, j))


def _epi_conv_in(ids, accs, extra, outs):
    b, c, v = accs
    outs[0][...] = b
    outs[1][...] = c * v


def _epi_residual(ids, accs, extra, outs):
    outs[0][...] = extra[0][...] + accs[0]


def _epi_swiglu(ids, accs, extra, outs):
    g, u = accs
    outs[0][...] = (g * _sigmoid(g) * u).astype(outs[0].dtype)


def _head_rmsnorm(a, gain):
    parts = []
    for h in range(a.shape[1] // HEAD_DIM):
        ah = a[:, h * HEAD_DIM:(h + 1) * HEAD_DIM]
        ms = jnp.mean(ah * ah, axis=-1, keepdims=True)
        parts.append(ah * lax.rsqrt(ms + EPS) * gain)
    return jnp.concatenate(parts, axis=1)


def _epi_kv(ids, accs, extra, outs):
    _, j = ids
    a = accs[0]
    normed = _head_rmsnorm(a, extra[0][...])
    y = jnp.where((j == 2) | (j == 4), normed, a)
    outs[0][...] = y
    outs[1][...] = y.astype(BF16)


def _epi_q(ids, accs, extra, outs):
    outs[0][...] = _head_rmsnorm(accs[0], extra[0][...]).astype(BF16)


def _epi_gate(ids, accs, extra, outs):
    outs[0][...] = _sigmoid(accs[0])


def _conv_kernel(u_ref, up_ref, b_ref, cw_ref, s1_ref, s2_ref, z_ref, *, tm, m_prompt, seq, dec_seq,
                 n_row_tiles):
    i = pl.program_id(0)
    u = u_ref[...]
    prev = up_ref[...]
    loc = lax.broadcasted_iota(jnp.int32, (tm, 1), 0)
    r = i * tm + loc
    u1 = pltpu.roll(u, 1, 0)
    u1 = jnp.where(loc == 0, prev[7:8, :], u1)
    u2 = pltpu.roll(u, 2, 0)
    u2 = jnp.where(loc == 0, prev[6:7, :], jnp.where(loc == 1, prev[7:8, :], u2))
    t = jnp.where(r < m_prompt, r & (seq - 1), (r - m_prompt) & (dec_seq - 1))
    u1 = jnp.where(t >= 1, u1, 0.0)
    u2 = jnp.where(t >= 2, u2, 0.0)
    w0 = cw_ref[0:1, :]
    w1 = cw_ref[1:2, :]
    w2 = cw_ref[2:3, :]
    conv = w2 * u + w1 * u1 + w0 * u2
    z_ref[...] = (b_ref[...] * conv).astype(z_ref.dtype)
    ms = s1_ref.shape[0]

    @pl.when(i == n_row_tiles - 1)
    def _():
        tail = conv[tm - ms:, :] + w1 * s1_ref[...] + w0 * s2_ref[...]
        z_ref[tm - ms:, :] = (b_ref[tm - ms:, :] * tail).astype(z_ref.dtype)


def _conv_gate(u, b, cw, s1, s2, *, m_prompt, seq, dec_seq):
    m, d = u.shape
    ms = s1.shape[0]
    tm = _row_tile(m)
    tc = _pick(d, (512, 256, 128))
    n_row_tiles = m // tm
    kern = functools.partial(_conv_kernel, tm=tm, m_prompt=m_prompt, seq=seq, dec_seq=dec_seq,
                             n_row_tiles=n_row_tiles)
    return pl.pallas_call(
        kern,
        out_shape=jax.ShapeDtypeStruct((m, d), BF16),
        grid=(n_row_tiles, d // tc),
        in_specs=[pl.BlockSpec((tm, tc), lambda i, j: (i, j)),
                  pl.BlockSpec((8, tc), lambda i, j: (jnp.maximum(i * (tm // 8) - 1, 0), j)),
                  pl.BlockSpec((tm, tc), lambda i, j: (i, j)),
                  pl.BlockSpec((CONV_W, tc), lambda i, j: (0, j)),
                  pl.BlockSpec((ms, tc), lambda i, j: (0, j)),
                  pl.BlockSpec((ms, tc), lambda i, j: (0, j))],
        out_specs=pl.BlockSpec((tm, tc), lambda i, j: (i, j)),
        compiler_params=_params(("parallel", "parallel")),
        name="conv_gate",
    )(u, u, b, cw, s1, s2)


def _cmp_stage1_compute(head_refs, w1_ref, pe_ref, top_ref, bot_ref, nch):
    for sec in range(2):
        wt = w1_ref[sec, 0].astype(BF16)
        wb = w1_ref[sec, 1].astype(BF16)
        pt = pe_ref[sec, 0]
        pb = pe_ref[sec, 1]
        rows = []
        for g in range(N_KV):
            ref = head_refs[sec * N_KV + g]
            cols = [ref[pl.ds(s, nch, stride=STRIDE), :] for s in range(STRIDE)]
            rows.append(jnp.concatenate(cols, axis=1))
        a = jnp.concatenate(rows, axis=0)
        top = jnp.dot((a + pt).astype(BF16), wt, preferred_element_type=F32)
        bot = jnp.dot((a + pb).astype(BF16), wb, preferred_element_type=F32)
        for g in range(N_KV):
            hh = sec * N_KV + g
            top_ref[:, hh * HEAD_DIM:(hh + 1) * HEAD_DIM] = top[g * nch:(g + 1) * nch, :]
            bot_ref[:, hh * HEAD_DIM:(hh + 1) * HEAD_DIM] = bot[g * nch:(g + 1) * nch, :]


def _cmp1_prompt_kernel(*refs, nch):
    n_heads = 2 * N_KV
    head_refs = refs[:n_heads]
    w1_ref, pe_ref, top_ref, bot_ref = refs[n_heads:]
    _cmp_stage1_compute(head_refs, w1_ref, pe_ref, top_ref, bot_ref, nch)


def _cmp1_sample_kernel(pt_ref, cache_ref, w1_ref, pe_ref, top_ref, bot_ref, buf_ref, sem, *,
                        pages_per_step, n_pages, page_size):
    b = pl.program_id(0)
    grp = pl.program_id(1)
    n_heads = 2 * N_KV

    def copy(p, hh):
        page = pt_ref[b * n_pages + grp * pages_per_step + p]
        return pltpu.make_async_copy(
            cache_ref.at[page, :, hh // N_KV, hh % N_KV, :],
            buf_ref.at[hh, pl.ds(p * page_size, page_size), :],
            sem.at[0])

    for p in range(pages_per_step):
        for hh in range(n_heads):
            copy(p, hh).start()
    for p in range(pages_per_step):
        for hh in range(n_heads):
            copy(p, hh).wait()
    _cmp_stage1_compute([buf_ref.at[hh] for hh in range(n_heads)], w1_ref, pe_ref, top_ref, bot_ref,
                        pages_per_step * page_size // STRIDE)


def _cmp2_kernel(top_ref, bot_ref, w2_ref, kn_ref, o_ref, *, nch):
    pre = top_ref[...] + pltpu.roll(bot_ref[...], nch - 1, 0)
    a = (pre * _sigmoid(pre)).astype(BF16)
    for sec in range(2):
        w2 = w2_ref[sec].astype(BF16)
        for g in range(N_KV):
            hh = sec * N_KV + g
            y = jnp.dot(a[:, hh * HEAD_DIM:(hh + 1) * HEAD_DIM], w2, preferred_element_type=F32)
            if sec == 0:
                y = _head_rmsnorm(y, kn_ref[...])
            o_ref[:, hh * HEAD_DIM:(hh + 1) * HEAD_DIM] = y.astype(o_ref.dtype)


def _cmp_stage2(top, bot, w2, kn0, nch):
    rows, width = top.shape
    return pl.pallas_call(
        functools.partial(_cmp2_kernel, nch=nch),
        out_shape=jax.ShapeDtypeStruct((rows, width), BF16),
        grid=(rows // nch,),
        in_specs=[pl.BlockSpec((nch, width), lambda b: (b, 0)),
                  pl.BlockSpec((nch, width), lambda b: (b, 0)),
                  pl.BlockSpec(w2.shape, lambda b: (0, 0, 0)),
                  pl.BlockSpec(kn0.shape, lambda b: (0, 0))],
        out_specs=pl.BlockSpec((nch, width), lambda b: (b, 0)),
        compiler_params=_params(("parallel",)),
        name="cmp_stage2",
    )(top, bot, w2, kn0)


def _compress_prompt(kv_f32, w1r, per, w2, kn0, *, batch, seq):
    width = 2 * N_KV * HEAD_DIM
    nch = seq // STRIDE
    top, bot = pl.pallas_call(
        functools.partial(_cmp1_prompt_kernel, nch=nch),
        out_shape=[jax.ShapeDtypeStruct((batch * nch, width), F32)] * 2,
        grid=(batch,),
        in_specs=[pl.BlockSpec((seq, HEAD_DIM), lambda b, hh=hh: (b, hh)) for hh in range(2 * N_KV)]
        + [pl.BlockSpec(w1r.shape, lambda b: (0, 0, 0, 0)),
           pl.BlockSpec(per.shape, lambda b: (0, 0, 0, 0))],
        out_specs=[pl.BlockSpec((nch, width), lambda b: (b, 0))] * 2,
        compiler_params=_params(("parallel",)),
        name="cmp_stage1_prompt",
    )(*([kv_f32] * (2 * N_KV)), w1r, per)
    return _cmp_stage2(top, bot, w2, kn0, nch)


def _compress_sample(cache2d, pt_flat, w1r, per, w2, kn0, *, dec_batch, n_pages, page_size):
    width = 2 * N_KV * HEAD_DIM
    pages_per_step = _pick(n_pages, (16, 8, 4, 2, 1))
    n_groups = n_pages // pages_per_step
    rows_step = pages_per_step * page_size
    nch_step = rows_step // STRIDE
    nch = n_pages * page_size // STRIDE
    kern = functools.partial(_cmp1_sample_kernel, pages_per_step=pages_per_step, n_pages=n_pages,
                             page_size=page_size)
    top, bot = pl.pallas_call(
        kern,
        out_shape=[jax.ShapeDtypeStruct((dec_batch * nch, width), F32)] * 2,
        grid_spec=pltpu.PrefetchScalarGridSpec(
            num_scalar_prefetch=1,
            grid=(dec_batch, n_groups),
            in_specs=[pl.BlockSpec(memory_space=pl.ANY),
                      pl.BlockSpec(w1r.shape, lambda b, g, pt: (0, 0, 0, 0)),
                      pl.BlockSpec(per.shape, lambda b, g, pt: (0, 0, 0, 0))],
            out_specs=[pl.BlockSpec((nch_step, width), lambda b, g, pt: (b * n_groups + g, 0))] * 2,
            scratch_shapes=[pltpu.VMEM((2 * N_KV, rows_step, HEAD_DIM), F32),
                            pltpu.SemaphoreType.DMA((1,))]),
        compiler_params=_params(("arbitrary", "arbitrary")),
        name="cmp_stage1_sample",
    )(pt_flat, cache2d, w1r, per)
    return _cmp_stage2(top, bot, w2, kn0, nch)


def _split3(x):
    hi = x.astype(BF16)
    r1 = x - hi.astype(F32)
    mid = r1.astype(BF16)
    lo = (r1 - mid.astype(F32)).astype(BF16)
    return hi, mid, lo


def _dot_f32ish(x, m_bf16):
    hi, mid, lo = _split3(x)
    out = jnp.dot(lo, m_bf16, preferred_element_type=F32)
    out = out + jnp.dot(mid, m_bf16, preferred_element_type=F32)
    return out + jnp.dot(hi, m_bf16, preferred_element_type=F32)


def _inter_t(n_c, n_s_lanes):
    c0 = lax.broadcasted_iota(jnp.int32, (n_c, n_s_lanes), 0) * STRIDE
    s0 = lax.broadcasted_iota(jnp.int32, (n_c, n_s_lanes), 1) * L_SLC
    return ((c0 < s0 + L_SLC) & (c0 + L_CMP > s0)).astype(BF16)


def _rank_desc(score, ns):
    lane = lax.broadcasted_iota(jnp.int32, score.shape, 1)
    rank = jnp.zeros(score.shape, jnp.int32)
    for sp in range(ns):
        col = score[:, sp:sp + 1]
        ahead = (col > score) | ((col == score) & (lane > sp))
        rank = rank + ahead.astype(jnp.int32)
    return rank


def _selection_scores(imp, qblk, ns):
    j = lax.broadcasted_iota(jnp.int32, imp.shape, 1)
    forced = (j == 0) | (j == qblk) | (j == qblk - 1)
    score = jnp.where(forced, FORCE, imp)
    score = jnp.where(j <= qblk, score, -FORCE)
    return jnp.where(j < ns, score, -2.0 * FORCE)


def _nsa_prompt_kernel(q_ref, g_ref, kc_ref, vc_ref, ks_ref, vs_ref, kw_ref, vw_ref, o_init_ref, o_ref,
                       m_sc, l_sc, acc_sc, *, tq, tk, hpg, seq, ncp):
    del o_init_ref
    qt = pl.program_id(2)
    t0 = qt * tq
    scale = HEAD_DIM ** -0.5
    rows = hpg * tq
    qf = q_ref[...]
    q = jnp.concatenate([qf[:, h * HEAD_DIM:(h + 1) * HEAD_DIM] for h in range(hpg)], axis=0)
    qpos = t0 + lax.broadcasted_iota(jnp.int32, (tq, 1), 0)
    nt = (((1,), (1,)), ((), ()))

    s = lax.dot_general(q, kc_ref[...], nt, preferred_element_type=F32) * scale
    c_end = lax.broadcasted_iota(jnp.int32, (tq, ncp), 1) * STRIDE + (L_CMP - 1)
    mc = (c_end <= qpos)[None]
    s3 = jnp.where(mc, s.reshape(hpg, tq, ncp), NEG)
    e = jnp.where(mc, jnp.exp(s3 - jnp.max(s3, axis=-1, keepdims=True)), 0.0)
    den = jnp.sum(e, axis=-1, keepdims=True)
    p_c = jnp.where(den > 0.0, e / jnp.where(den > 0.0, den, 1.0), 0.0)
    o_c = jnp.dot(p_c.reshape(rows, ncp).astype(BF16), vc_ref[...], preferred_element_type=F32)
    psum = jnp.sum(p_c, axis=0)

    ns = -(-seq // L_SLC)
    imp = _dot_f32ish(psum, _inter_t(ncp, LANES))
    qblk = _div_pow2(qpos, L_SLC)
    score = _selection_scores(imp, qblk, ns)
    rank = _rank_desc(score, ns)
    sel = ((rank < N_SEL) & (score > -0.5 * FORCE)).astype(BF16)

    m_sc[...] = jnp.full(m_sc.shape, NEG, F32)
    l_sc[...] = jnp.zeros(l_sc.shape, F32)
    acc_sc[...] = jnp.zeros(acc_sc.shape, F32)

    def body(j, carry):
        k0 = pl.multiple_of(j * tk, tk)
        kt = ks_ref[pl.ds(k0, tk), :]
        vt = vs_ref[pl.ds(k0, tk), :]
        sj = lax.dot_general(q, kt, nt, preferred_element_type=F32) * scale
        kpos = k0 + lax.broadcasted_iota(jnp.int32, (1, tk), 1)
        expand = (lax.broadcasted_iota(jnp.int32, (LANES, tk), 0) == _div_pow2(kpos, L_SLC)).astype(BF16)
        km = jnp.dot(sel, expand, preferred_element_type=F32) > 0.5
        mask = (km & (kpos <= qpos))[None]
        sj3 = jnp.where(mask, sj.reshape(hpg, tq, tk), NEG)
        m_old = m_sc[...]
        m_new = jnp.maximum(m_old, jnp.max(sj3, axis=-1, keepdims=True))
        p = jnp.where(mask, jnp.exp(sj3 - m_new), 0.0)
        alpha = jnp.exp(m_old - m_new)
        l_sc[...] = alpha * l_sc[...] + jnp.sum(p, axis=-1, keepdims=True)
        pv = jnp.dot(p.reshape(rows, tk).astype(BF16), vt, preferred_element_type=F32)
        acc_sc[...] = alpha * acc_sc[...] + pv.reshape(hpg, tq, HEAD_DIM)
        m_sc[...] = m_new
        return carry

    lax.fori_loop(0, (t0 + tq - 1) // tk + 1, body, 0)
    o_s = acc_sc[...] / l_sc[...]

    wl = WINDOW + tq
    w0 = pl.multiple_of(jnp.clip(t0 - WINDOW, 0, seq - wl), tq)
    kwt = kw_ref[pl.ds(w0, wl), :]
    vwt = vw_ref[pl.ds(w0, wl), :]
    sw = lax.dot_general(q, kwt, nt, preferred_element_type=F32) * scale
    wpos = w0 + lax.broadcasted_iota(jnp.int32, (1, wl), 1)
    dlt = qpos - wpos
    mw = ((dlt >= 0) & (dlt <= WINDOW))[None]
    sw3 = jnp.where(mw, sw.reshape(hpg, tq, wl), NEG)
    ew = jnp.where(mw, jnp.exp(sw3 - jnp.max(sw3, axis=-1, keepdims=True)), 0.0)
    lw = jnp.sum(ew, axis=-1, keepdims=True)
    o_w = jnp.dot(ew.reshape(rows, wl).astype(BF16), vwt, preferred_element_type=F32)
    o_w = o_w.reshape(hpg, tq, HEAD_DIM) / lw

    gt = g_ref[...]
    o_c3 = o_c.reshape(hpg, tq, HEAD_DIM)
    for h in range(hpg):
        g0 = gt[:, 0 * hpg + h:0 * hpg + h + 1]
        g1 = gt[:, 1 * hpg + h:1 * hpg + h + 1]
        g2 = gt[:, 2 * hpg + h:2 * hpg + h + 1]
        o_ref[:, h * HEAD_DIM:(h + 1) * HEAD_DIM] = (
            g0 * o_c3[h] + g1 * o_s[h] + g2 * o_w[h]).astype(o_ref.dtype)


def _nsa_prompt(q, gates, kc, kv_bf, *, m_total, batch, seq, hpg):
    d = q.shape[1]
    tq = 128
    tk = _pick(seq, (512, 256, 128))
    ncp = seq // STRIDE
    nqt = seq // tq
    gw = hpg * HEAD_DIM
    rows = hpg * tq
    kern = functools.partial(_nsa_prompt_kernel, tq=tq, tk=tk, hpg=hpg, seq=seq, ncp=ncp)

    def kvspec(sec):
        return pl.BlockSpec((seq, HEAD_DIM), lambda b, g, t, sec=sec: (b, sec * N_KV + g))

    return pl.pallas_call(
        kern,
        out_shape=jax.ShapeDtypeStruct((m_total, d), BF16),
        grid=(batch, N_KV, nqt),
        in_specs=[pl.BlockSpec((tq, gw), lambda b, g, t: (b * nqt + t, g)),
                  pl.BlockSpec((tq, LANES), lambda b, g, t: (b * nqt + t, g)),
                  pl.BlockSpec((ncp, HEAD_DIM), lambda b, g, t: (b, g)),
                  pl.BlockSpec((ncp, HEAD_DIM), lambda b, g, t: (b, N_KV + g)),
                  kvspec(2), kvspec(3), kvspec(4), kvspec(5),
                  pl.BlockSpec(memory_space=pl.ANY)],
        out_specs=pl.BlockSpec((tq, gw), lambda b, g, t: (b * nqt + t, g)),
        scratch_shapes=[pltpu.VMEM((hpg, tq, 1), F32), pltpu.VMEM((hpg, tq, 1), F32),
                        pltpu.VMEM((hpg, tq, HEAD_DIM), F32)],
        input_output_aliases={8: 0},
        compiler_params=_params(("parallel", "parallel", "arbitrary")),
        name="nsa_prompt",
    )(q, gates, kc, kc, kv_bf, kv_bf, kv_bf, kv_bf, jnp.zeros((m_total, d), BF16))


def _nsa_sample_select_kernel(q_ref, kc_ref, vc_ref, oc_ref, idx_ref, *, dec_batch, dec_seq, hpg,
                              past_len, ncp, ns, ns_lanes):
    scale = HEAD_DIM ** -0.5
    nt = (((1,), (1,)), ((), ()))
    qf = q_ref[...].astype(F32)
    rows = hpg * dec_seq
    qrow = lax.broadcasted_iota(jnp.int32, (dec_seq, 1), 0)
    qpos = past_len + qrow
    c_end = lax.broadcasted_iota(jnp.int32, (dec_seq, ncp), 1) * STRIDE + (L_CMP - 1)
    mc = (c_end <= qpos)[None]
    psums = []
    for b in range(dec_batch):
        qb = qf[b * dec_seq:(b + 1) * dec_seq, :]
        q = jnp.concatenate([qb[:, h * HEAD_DIM:(h + 1) * HEAD_DIM] for h in range(hpg)],
                            axis=0).astype(BF16)
        kc = kc_ref[b * ncp:(b + 1) * ncp, :]
        vc = vc_ref[b * ncp:(b + 1) * ncp, :]
        s = lax.dot_general(q, kc, nt, preferred_element_type=F32) * scale
        s3 = jnp.where(mc, s.reshape(hpg, dec_seq, ncp), NEG)
        e = jnp.where(mc, jnp.exp(s3 - jnp.max(s3, axis=-1, keepdims=True)), 0.0)
        den = jnp.sum(e, axis=-1, keepdims=True)
        p_c = jnp.where(den > 0.0, e / jnp.where(den > 0.0, den, 1.0), 0.0)
        o_c = jnp.dot(p_c.reshape(rows, ncp).astype(BF16), vc, preferred_element_type=F32)
        for h in range(hpg):
            oc_ref[b * dec_seq:(b + 1) * dec_seq, h * HEAD_DIM:(h + 1) * HEAD_DIM] = (
                o_c[h * dec_seq:(h + 1) * dec_seq, :])
        psums.append(jnp.sum(p_c, axis=0))
    psum = jnp.concatenate(psums, axis=0)
    imp = _dot_f32ish(psum, _inter_t(ncp, ns_lanes))
    n_rows = dec_batch * dec_seq
    qpos_all = past_len + (lax.broadcasted_iota(jnp.int32, (n_rows, 1), 0) & (dec_seq - 1))
    score = _selection_scores(imp, _div_pow2(qpos_all, L_SLC), ns)
    rank = _rank_desc(score, ns)
    ok = score > -0.5 * FORCE
    lane = lax.broadcasted_iota(jnp.int32, score.shape, 1)
    out_lane = lax.broadcasted_iota(jnp.int32, (n_rows, LANES), 1)
    idx = jnp.full((n_rows, LANES), -1, jnp.int32)
    for k in range(N_SEL):
        hit = (rank == k) & ok
        blk = jnp.sum(jnp.where(hit, (lane + 1).astype(F32), 0.0), axis=-1, keepdims=True)
        blk = blk.astype(jnp.int32) - 1
        idx = jnp.where(out_lane == k, blk, idx)
    idx_ref[...] = idx


def _nsa_sample_attend_kernel(idx_ref, pt_ref, q_ref, g_ref, oc_ref, knew_ref, vnew_ref, kwo_ref, vwo_ref,
                              kwn_ref, vwn_ref, cache_ref, o_in_ref, o_ref, kbuf, vbuf, of32, sems, *,
                              dec_batch, dec_seq, hpg, past_len, n_pages, page_size, l_win):
    del o_in_ref
    g = pl.program_id(0)
    scale = HEAD_DIM ** -0.5
    nt = (((1,), (1,)), ((), ()))
    rows = hpg * dec_seq
    n_past_blocks = past_len // L_SLC
    blocks_per_page = page_size // L_SLC
    nkeys = N_SEL * L_SLC
    qf = q_ref[...].astype(F32)
    gt = g_ref[...]
    qrow = lax.broadcasted_iota(jnp.int32, (rows, 1), 0) & (dec_seq - 1)
    key_slot = _div_pow2(lax.broadcasted_iota(jnp.int32, (1, nkeys), 1), L_SLC)
    newj = lax.broadcasted_iota(jnp.int32, (1, dec_seq), 1)

    for b in range(dec_batch):
        def copies(qi, k):
            blk = idx_ref[((g * dec_batch + b) * dec_seq + qi) * N_SEL + k]
            blk = jnp.clip(blk, 0, n_past_blocks - 1)
            page = pt_ref[b * n_pages + blk // blocks_per_page]
            row0 = pl.multiple_of((blk % blocks_per_page) * L_SLC, L_SLC)
            ck = pltpu.make_async_copy(cache_ref.at[page, pl.ds(row0, L_SLC), 2, g, :],
                                       kbuf.at[qi, pl.ds(k * L_SLC, L_SLC), :], sems.at[0])
            cv = pltpu.make_async_copy(cache_ref.at[page, pl.ds(row0, L_SLC), 3, g, :],
                                       vbuf.at[qi, pl.ds(k * L_SLC, L_SLC), :], sems.at[1])
            return ck, cv

        for qi in range(dec_seq):
            for k in range(N_SEL):
                ck, cv = copies(qi, k)
                ck.start()
                cv.start()
        for qi in range(dec_seq):
            for k in range(N_SEL):
                ck, cv = copies(qi, k)
                ck.wait()
                cv.wait()

        qb = qf[b * dec_seq:(b + 1) * dec_seq, :]
        q = jnp.concatenate([qb[:, h * HEAD_DIM:(h + 1) * HEAD_DIM] for h in range(hpg)],
                            axis=0).astype(BF16)
        knew = knew_ref[b * dec_seq:(b + 1) * dec_seq, :].astype(BF16)
        vnew = vnew_ref[b * dec_seq:(b + 1) * dec_seq, :].astype(BF16)
        s_new = lax.dot_general(q, knew, nt, preferred_element_type=F32) * scale

        o_s = jnp.zeros((rows, HEAD_DIM), F32)
        for qi in range(dec_seq):
            valid = jnp.zeros((1, nkeys), jnp.int32)
            has_new = jnp.int32(0)
            for k in range(N_SEL):
                blk = idx_ref[((g * dec_batch + b) * dec_seq + qi) * N_SEL + k]
                is_past = ((blk >= 0) & (blk < n_past_blocks)).astype(jnp.int32)
                valid = jnp.where(key_slot == k, is_past, valid)
                has_new = has_new | (blk == n_past_blocks).astype(jnp.int32)
            mk = valid > 0
            mn = ((newj <= qi).astype(jnp.int32) * has_new) > 0
            kq = kbuf[qi].astype(BF16)
            vq = vbuf[qi].astype(BF16)
            s_old = lax.dot_general(q, kq, nt, preferred_element_type=F32) * scale
            s_old = jnp.where(mk, s_old, NEG)
            s_n = jnp.where(mn, s_new, NEG)
            mx = jnp.maximum(jnp.max(s_old, axis=-1, keepdims=True), jnp.max(s_n, axis=-1, keepdims=True))
            p_old = jnp.where(mk, jnp.exp(s_old - mx), 0.0)
            p_n = jnp.where(mn, jnp.exp(s_n - mx), 0.0)
            den = jnp.sum(p_old, axis=-1, keepdims=True) + jnp.sum(p_n, axis=-1, keepdims=True)
            o = (jnp.dot(p_old.astype(BF16), vq, preferred_element_type=F32)
                 + jnp.dot(p_n.astype(BF16), vnew, preferred_element_type=F32)) / den
            o_s = jnp.where(qrow == qi, o, o_s)

        kwo = kwo_ref[b * l_win:(b + 1) * l_win, :].astype(BF16)
        vwo = vwo_ref[b * l_win:(b + 1) * l_win, :].astype(BF16)
        kwn = kwn_ref[b * dec_seq:(b + 1) * dec_seq, :].astype(BF16)
        vwn = vwn_ref[b * dec_seq:(b + 1) * dec_seq, :].astype(BF16)
        qpos = past_len + qrow
        wpos_o = past_len - l_win + lax.broadcasted_iota(jnp.int32, (1, l_win), 1)
        wpos_n = past_len + newj
        d_o = qpos - wpos_o
        d_n = qpos - wpos_n
        m_o = (d_o >= 0) & (d_o <= WINDOW) & (wpos_o >= 0)
        m_n = (d_n >= 0) & (d_n <= WINDOW)
        sw_o = jnp.where(m_o, lax.dot_general(q, kwo, nt, preferred_element_type=F32) * scale, NEG)
        sw_n = jnp.where(m_n, lax.dot_general(q, kwn, nt, preferred_element_type=F32) * scale, NEG)
        mx = jnp.maximum(jnp.max(sw_o, axis=-1, keepdims=True), jnp.max(sw_n, axis=-1, keepdims=True))
        pw_o = jnp.where(m_o, jnp.exp(sw_o - mx), 0.0)
        pw_n = jnp.where(m_n, jnp.exp(sw_n - mx), 0.0)
        den = jnp.sum(pw_o, axis=-1, keepdims=True) + jnp.sum(pw_n, axis=-1, keepdims=True)
        o_w = (jnp.dot(pw_o.astype(BF16), vwo, preferred_element_type=F32)
               + jnp.dot(pw_n.astype(BF16), vwn, preferred_element_type=F32)) / den

        gb = gt[b * dec_seq:(b + 1) * dec_seq, :]
        for h in range(hpg):
            sl = slice(h * dec_seq, (h + 1) * dec_seq)
            g0 = gb[:, 0 * hpg + h:0 * hpg + h + 1]
            g1 = gb[:, 1 * hpg + h:1 * hpg + h + 1]
            g2 = gb[:, 2 * hpg + h:2 * hpg + h + 1]
            o_c = oc_ref[b * dec_seq:(b + 1) * dec_seq, h * HEAD_DIM:(h + 1) * HEAD_DIM]
            of32[b * dec_seq:(b + 1) * dec_seq, h * HEAD_DIM:(h + 1) * HEAD_DIM] = (
                g0 * o_c + g1 * o_s[sl, :] + g2 * o_w[sl, :])
    o_ref[...] = of32[...].astype(o_ref.dtype)


def _nsa_sample(q, gates, kc_s, kv_f32, cache2d, win2d, pt_flat, o_prompt, *, m_prompt, dec_batch,
                dec_seq, hpg, past_len, n_pages, page_size, l_win):
    ms = dec_batch * dec_seq
    gw = hpg * HEAD_DIM
    ncp = past_len // STRIDE
    t_all = past_len + dec_seq
    ns = -(-t_all // L_SLC)
    ns_lanes = -(-ns // LANES) * LANES
    rb = m_prompt // ms
    sel = functools.partial(_nsa_sample_select_kernel, dec_batch=dec_batch, dec_seq=dec_seq, hpg=hpg,
                            past_len=past_len, ncp=ncp, ns=ns, ns_lanes=ns_lanes)
    oc, idx = pl.pallas_call(
        sel,
        out_shape=[jax.ShapeDtypeStruct((ms, N_KV * gw), F32),
                   jax.ShapeDtypeStruct((N_KV * ms, LANES), jnp.int32)],
        grid=(N_KV,),
        in_specs=[pl.BlockSpec((ms, gw), lambda g: (rb, g)),
                  pl.BlockSpec((dec_batch * ncp, HEAD_DIM), lambda g: (0, g)),
                  pl.BlockSpec((dec_batch * ncp, HEAD_DIM), lambda g: (0, N_KV + g))],
        out_specs=[pl.BlockSpec((ms, gw), lambda g: (0, g)),
                   pl.BlockSpec((ms, LANES), lambda g: (g, 0))],
        compiler_params=_params(("parallel",)),
        name="nsa_sample_select",
    )(q, kc_s, kc_s)
    idx_flat = idx[:, :N_SEL].reshape(-1)

    att = functools.partial(_nsa_sample_attend_kernel, dec_batch=dec_batch, dec_seq=dec_seq, hpg=hpg,
                            past_len=past_len, n_pages=n_pages, page_size=page_size, l_win=l_win)

    def sm(f):
        return lambda g, idx_r, pt_r: f(g)

    return pl.pallas_call(
        att,
        out_shape=jax.ShapeDtypeStruct(o_prompt.shape, o_prompt.dtype),
        grid_spec=pltpu.PrefetchScalarGridSpec(
            num_scalar_prefetch=2,
            grid=(N_KV,),
            in_specs=[pl.BlockSpec((ms, gw), sm(lambda g: (rb, g))),
                      pl.BlockSpec((ms, LANES), sm(lambda g: (rb, g))),
                      pl.BlockSpec((ms, gw), sm(lambda g: (0, g))),
                      pl.BlockSpec((ms, HEAD_DIM), sm(lambda g: (rb, 2 * N_KV + g))),
                      pl.BlockSpec((ms, HEAD_DIM), sm(lambda g: (rb, 3 * N_KV + g))),
                      pl.BlockSpec((dec_batch * l_win, HEAD_DIM), sm(lambda g: (0, g))),
                      pl.BlockSpec((dec_batch * l_win, HEAD_DIM), sm(lambda g: (0, N_KV + g))),
                      pl.BlockSpec((ms, HEAD_DIM), sm(lambda g: (rb, 4 * N_KV + g))),
                      pl.BlockSpec((ms, HEAD_DIM), sm(lambda g: (rb, 5 * N_KV + g))),
                      pl.BlockSpec(memory_space=pl.ANY),
                      pl.BlockSpec(memory_space=pl.ANY)],
            out_specs=pl.BlockSpec((ms, gw), sm(lambda g: (rb, g))),
            scratch_shapes=[pltpu.VMEM((dec_seq, N_SEL * L_SLC, HEAD_DIM), F32),
                            pltpu.VMEM((dec_seq, N_SEL * L_SLC, HEAD_DIM), F32),
                            pltpu.VMEM((ms, gw), F32),
                            pltpu.SemaphoreType.DMA((2,))]),
        input_output_aliases={12: 0},
        compiler_params=_params(("arbitrary",)),
        name="nsa_sample_attend",
    )(idx_flat, pt_flat, q, gates, oc, kv_f32, kv_f32, win2d, win2d, kv_f32, kv_f32, cache2d, o_prompt)


def _router_kernel(x_ref, g_ref, r_ref, hp_ref, idx_ref, wts_ref, *, n_experts):
    x = x_ref[...]
    h = x * lax.rsqrt(jnp.mean(x * x, axis=-1, keepdims=True) + EPS) * g_ref[...]
    half = h.shape[1] // 2
    bits = pltpu.bitcast(h.astype(BF16).astype(F32), jnp.uint32)
    hp_ref[...] = (bits[:, :half] >> 16) | (bits[:, half:] & jnp.uint32(0xFFFF0000))
    r_hi, r_mid, r_lo = _split3(r_ref[...])
    h_hi, h_mid, h_lo = _split3(h)

    def d(a, b):
        return jnp.dot(a, b, preferred_element_type=F32)

    logits = (d(h_lo, r_hi) + d(h_hi, r_lo) + d(h_mid, r_mid)) + (d(h_mid, r_hi) + d(h_hi, r_mid)) + d(h_hi, r_hi)
    lane = lax.broadcasted_iota(jnp.int32, logits.shape, 1)
    logits = jnp.where(lane < n_experts, logits, -jnp.inf)
    v1 = jnp.max(logits, axis=-1, keepdims=True)
    i1 = jnp.min(jnp.where(logits == v1, lane, LANES), axis=-1, keepdims=True)
    rest = jnp.where(lane == i1, -jnp.inf, logits)
    v2 = jnp.max(rest, axis=-1, keepdims=True)
    i2 = jnp.min(jnp.where(rest == v2, lane, LANES), axis=-1, keepdims=True)
    e2 = jnp.exp(v2 - v1)
    w1 = 1.0 / (1.0 + e2)
    w2 = e2 / (1.0 + e2)
    idx_ref[...] = jnp.where(lane == 0, i1, jnp.where(lane == 1, i2, 0))
    wts_ref[...] = jnp.where(lane == 0, w1, jnp.where(lane == 1, w2, 0.0))


def _router(x, gain, router_p, n_experts):
    m, d = x.shape
    tm = _pick(m, (192, 128, 64, 32, 16))
    return pl.pallas_call(
        functools.partial(_router_kernel, n_experts=n_experts),
        out_shape=[jax.ShapeDtypeStruct((m, d // 2), jnp.uint32),
                   jax.ShapeDtypeStruct((m, LANES), jnp.int32),
                   jax.ShapeDtypeStruct((m, LANES), F32)],
        grid=(m // tm,),
        in_specs=[pl.BlockSpec((tm, d), lambda i: (i, 0)),
                  pl.BlockSpec((1, d), lambda i: (0, 0)),
                  pl.BlockSpec((d, LANES), lambda i: (0, 0))],
        out_specs=[pl.BlockSpec((tm, d // 2), lambda i: (i, 0)),
                   pl.BlockSpec((tm, LANES), lambda i: (i, 0)),
                   pl.BlockSpec((tm, LANES), lambda i: (i, 0))],
        compiler_params=_params(("parallel",)),
        name="moe_router",
    )(x, gain, router_p)


def _moe_plan(idx2, n_experts, rc, nch, sub):
    e = idx2.reshape(-1)
    onehot = (e[:, None] == jnp.arange(n_experts, dtype=jnp.int32)[None, :]).astype(jnp.int32)
    csum = jnp.cumsum(onehot, axis=0)
    rank = jnp.take_along_axis(csum, e[:, None], axis=1)[:, 0] - 1
    counts = csum[-1]
    nchunks = (counts + rc - 1) // rc
    cend = jnp.cumsum(nchunks)
    cstart = cend - nchunks
    pos = (cstart[e] + rank // rc) * rc + rank % rc
    n_active = cend[-1]
    c = jnp.arange(nch, dtype=jnp.int32)
    cx = jnp.minimum(c, n_active - 1)
    ce = jnp.searchsorted(cend, cx, side="right").astype(jnp.int32)
    rows = jnp.clip(counts[ce] - (cx - cstart[ce]) * rc, 0, rc)
    nsub = jnp.where(c < n_active, (rows + sub - 1) // sub, 0)
    return pos.astype(jnp.int32), nsub.astype(jnp.int32), cx.astype(jnp.int32), ce


def _moe_scatter_kernel(pos_ref, hp_ref, xs_in_ref, xs_ref, sems, *, tt):
    del xs_in_ref
    i = pl.program_id(0)

    def copies(r):
        t = i * tt + r
        return [pltpu.make_async_copy(hp_ref.at[pl.ds(t, 1), :],
                                      xs_ref.at[pl.ds(pos_ref[0, 0, TOP_K * r + s], 1), :], sems.at[s])
                for s in range(TOP_K)]

    def start(r, carry):
        for cp in copies(r):
            cp.start()
        return carry

    def wait(r, carry):
        for cp in copies(r):
            cp.wait()
        return carry

    lax.fori_loop(0, tt, start, 0)
    lax.fori_loop(0, tt, wait, 0)


def _moe_scatter(hp, pos, n_rows):
    m, dp = hp.shape
    tt = _pick(m, (192, 128, 64, 32, 16))
    pos3 = pos.reshape(m // tt, 1, TOP_K * tt)
    return pl.pallas_call(
        functools.partial(_moe_scatter_kernel, tt=tt),
        out_shape=jax.ShapeDtypeStruct((n_rows, dp), hp.dtype),
        grid=(m // tt,),
        in_specs=[pl.BlockSpec((1, 1, TOP_K * tt), lambda i: (i, 0, 0), memory_space=pltpu.SMEM),
                  pl.BlockSpec(memory_space=pl.ANY),
                  pl.BlockSpec(memory_space=pl.ANY)],
        out_specs=pl.BlockSpec(memory_space=pl.ANY),
        scratch_shapes=[pltpu.SemaphoreType.DMA((TOP_K,))],
        input_output_aliases={2: 0},
        compiler_params=_params(("arbitrary",)),
        name="moe_scatter",
    )(pos3, hp, jnp.zeros((n_rows, dp), hp.dtype))


def _unpack_bf16_pair(xp):
    lo = pltpu.bitcast(xp << 16, F32).astype(BF16)
    hi = pltpu.bitcast(xp & jnp.uint32(0xFFFF0000), F32).astype(BF16)
    return lo, hi


def _moe_gu_kernel(nsub_ref, cx_ref, ce_ref, x_ref, wgl_ref, wgh_ref, wul_ref, wuh_ref, o_ref,
                   accg, accu, wb, *, nk, sub):
    del cx_ref, ce_ref
    c = pl.program_id(0)
    k = pl.program_id(2)
    ns = nsub_ref[c]

    @pl.when(ns > 0)
    def _():
        @pl.when(k == 0)
        def _():
            accg[...] = jnp.zeros_like(accg)
            accu[...] = jnp.zeros_like(accu)

        for n, w_ref in enumerate((wgl_ref, wgh_ref, wul_ref, wuh_ref)):
            wb[n] = w_ref[...].astype(BF16)

        def body(s, carry):
            rows = pl.ds(pl.multiple_of(s * sub, sub), sub)
            lo, hi = _unpack_bf16_pair(x_ref[rows, :])
            accg[rows, :] += (jnp.dot(lo, wb[0], preferred_element_type=F32)
                              + jnp.dot(hi, wb[1], preferred_element_type=F32))
            accu[rows, :] += (jnp.dot(lo, wb[2], preferred_element_type=F32)
                              + jnp.dot(hi, wb[3], preferred_element_type=F32))
            return carry

        lax.fori_loop(0, ns, body, 0)

        @pl.when(k == nk - 1)
        def _():
            g = accg[...]
            o_ref[...] = (g * _sigmoid(g) * accu[...]).astype(o_ref.dtype)


def _moe_down_kernel(nsub_ref, cx_ref, ce_ref, x_ref, w_ref, o_ref, wb, *, sub):
    del cx_ref, ce_ref
    c = pl.program_id(0)
    k = pl.program_id(2)
    ns = nsub_ref[c]

    @pl.when(ns > 0)
    def _():
        @pl.when(k == 0)
        def _():
            o_ref[...] = jnp.zeros_like(o_ref)

        wb[...] = w_ref[...].astype(BF16)

        def body(s, carry):
            rows = pl.ds(pl.multiple_of(s * sub, sub), sub)
            o_ref[rows, :] += jnp.dot(x_ref[rows, :], wb[...], preferred_element_type=F32)
            return carry

        lax.fori_loop(0, ns, body, 0)


def _moe_experts(xs, wgu, wdn, nsub, cx, ce, *, d, d_ff, rc, nch, sub):
    half = d // 2
    tkp = _pick(half, (512, 256, 128))
    nk = half // tkp
    tn = _pick(d_ff, (512, 256, 128))
    nj = d_ff // tn

    def frozen(c, a, last, nsub_r):
        return jnp.where(nsub_r[c] > 0, a, last)

    def x_map(c, j, k, nsub_r, cx_r, ce_r):
        return (cx_r[c], frozen(c, k, nk - 1, nsub_r))

    def w_map(col_off, row_off):
        def f(c, j, k, nsub_r, cx_r, ce_r):
            return (ce_r[c] * (d // tkp) + row_off // tkp + frozen(c, k, nk - 1, nsub_r),
                    col_off // tn + frozen(c, j, nj - 1, nsub_r))
        return f

    def o_map(c, j, k, nsub_r, cx_r, ce_r):
        return (cx_r[c], frozen(c, j, nj - 1, nsub_r))

    act = pl.pallas_call(
        functools.partial(_moe_gu_kernel, nk=nk, sub=sub),
        out_shape=jax.ShapeDtypeStruct((nch * rc, d_ff), BF16),
        grid_spec=pltpu.PrefetchScalarGridSpec(
            num_scalar_prefetch=3,
            grid=(nch, nj, nk),
            in_specs=[pl.BlockSpec((rc, tkp), x_map),
                      pl.BlockSpec((tkp, tn), w_map(0, 0)),
                      pl.BlockSpec((tkp, tn), w_map(0, half)),
                      pl.BlockSpec((tkp, tn), w_map(d_ff, 0)),
                      pl.BlockSpec((tkp, tn), w_map(d_ff, half))],
            out_specs=pl.BlockSpec((rc, tn), o_map),
            scratch_shapes=[pltpu.VMEM((rc, tn), F32), pltpu.VMEM((rc, tn), F32),
                            pltpu.VMEM((4, tkp, tn), BF16)]),
        compiler_params=_params(("arbitrary", "arbitrary", "arbitrary")),
        name="moe_gu",
    )(nsub, cx, ce, xs, wgu, wgu, wgu, wgu)

    tk2 = _pick(d_ff, (1024, 512, 256, 128))
    nk2 = d_ff // tk2
    tn2 = _pick(d, (1024, 512, 256, 128))
    nj2 = d // tn2

    def x2_map(c, j, k, nsub_r, cx_r, ce_r):
        return (cx_r[c], frozen(c, k, nk2 - 1, nsub_r))

    def w2_map(c, j, k, nsub_r, cx_r, ce_r):
        return (ce_r[c] * nk2 + frozen(c, k, nk2 - 1, nsub_r), frozen(c, j, nj2 - 1, nsub_r))

    def o2_map(c, j, k, nsub_r, cx_r, ce_r):
        return (cx_r[c], frozen(c, j, nj2 - 1, nsub_r))

    return pl.pallas_call(
        functools.partial(_moe_down_kernel, sub=sub),
        out_shape=jax.ShapeDtypeStruct((nch * rc, d), F32),
        grid_spec=pltpu.PrefetchScalarGridSpec(
            num_scalar_prefetch=3,
            grid=(nch, nj2, nk2),
            in_specs=[pl.BlockSpec((rc, tk2), x2_map),
                      pl.BlockSpec((tk2, tn2), w2_map)],
            out_specs=pl.BlockSpec((rc, tn2), o2_map),
            scratch_shapes=[pltpu.VMEM((tk2, tn2), BF16)]),
        compiler_params=_params(("arbitrary", "arbitrary", "arbitrary")),
        name="moe_down",
    )(nsub, cx, ce, act, wdn)


def _moe_combine_kernel(pos_ref, x_ref, wts_ref, y_ref, o_ref, ybuf, sems, *, tt):
    i = pl.program_id(0)
    del i

    def copies(r):
        return [pltpu.make_async_copy(y_ref.at[pl.ds(pos_ref[0, 0, TOP_K * r + s], 1), :],
                                      ybuf.at[s, pl.ds(r, 1), :], sems.at[s])
                for s in range(TOP_K)]

    def start(r, carry):
        for cp in copies(r):
            cp.start()
        return carry

    def wait(r, carry):
        for cp in copies(r):
            cp.wait()
        return carry

    lax.fori_loop(0, tt, start, 0)
    lax.fori_loop(0, tt, wait, 0)
    w = wts_ref[...]
    o_ref[...] = x_ref[...] + (w[:, 0:1] * ybuf[0] + w[:, 1:2] * ybuf[1])


def _moe_combine(x, wts, y, pos):
    m, d = x.shape
    tt = _pick(m, (192, 128, 64, 32, 16))
    pos3 = pos.reshape(m // tt, 1, TOP_K * tt)
    return pl.pallas_call(
        functools.partial(_moe_combine_kernel, tt=tt),
        out_shape=jax.ShapeDtypeStruct((m, d), F32),
        grid=(m // tt,),
        in_specs=[pl.BlockSpec((1, 1, TOP_K * tt), lambda i: (i, 0, 0), memory_space=pltpu.SMEM),
                  pl.BlockSpec((tt, d), lambda i: (i, 0)),
                  pl.BlockSpec((tt, LANES), lambda i: (i, 0)),
                  pl.BlockSpec(memory_space=pl.ANY)],
        out_specs=pl.BlockSpec((tt, d), lambda i: (i, 0)),
        scratch_shapes=[pltpu.VMEM((TOP_K, tt, d), F32), pltpu.SemaphoreType.DMA((TOP_K,))],
        compiler_params=_params(("arbitrary",)),
        name="moe_combine",
    )(pos3, x, wts, y)


def kernel(x_prompt, x_sample, state_conv, cache_kv, cache_win, page_table, norm_mix, norm_ffn, conv_w_in, conv_w, conv_w_out, ffn_w_gu, ffn_w_down, moe_router, moe_w_gu, moe_w_down, kv_norm, w_kv, k_norm, cmp_w1, cmp_w2, cmp_pe, w_qg, q_norm, w_o):
    batch, seq, d = x_prompt.shape
    dec_batch, dec_seq, _ = x_sample.shape
    n_pool, page_size = cache_kv.shape[:2]
    n_pages = page_table.shape[1]
    past_len = n_pages * page_size
    l_win = cache_win.shape[1]
    d_ff = ffn_w_down.shape[1]
    n_experts = moe_router.shape[2]
    n_heads = d // HEAD_DIM
    hpg = n_heads // N_KV
    sec_w = N_KV * HEAD_DIM
    m_prompt = batch * seq
    ms = dec_batch * dec_seq
    m = m_prompt + ms
    assert seq & (seq - 1) == 0 and dec_seq & (dec_seq - 1) == 0 and dec_seq >= CONV_W - 1
    assert m_prompt % ms == 0 and ms % 16 == 0 and seq % 128 == 0 and seq >= WINDOW + 128
    assert past_len % L_SLC == 0 and dec_seq <= L_SLC and page_size % L_SLC == 0
    assert norm_mix.shape[0] == 2 and l_win == WINDOW and past_len >= l_win

    x0 = jnp.concatenate([x_prompt.reshape(m_prompt, d), x_sample.reshape(ms, d)], axis=0)
    tm = _row_tile(m)

    (h0,) = _rmsnorm(x0, norm_mix[0:1])
    tn = _pick(d, (256, 128))
    tk = _pick(d, (1024, 512, 256, 128))
    b_gate, u = _matmul(
        h0, conv_w_in[0], col_offsets=(0, d, 2 * d), n_cols=d, tm=tm, tn=tn, tk=tk,
        epilogue=_epi_conv_in, out_shapes=[jax.ShapeDtypeStruct((m, d), F32)] * 2,
        out_specs=[_ij_spec(tm, tn)] * 2, name="conv_in")
    st = state_conv[0]
    zrow = jnp.zeros((dec_batch, dec_seq - 1, d), F32)
    s1 = jnp.concatenate([st[:, 1:2], zrow], axis=1).reshape(ms, d)
    s2 = jnp.concatenate([st[:, 0:1], st[:, 1:2], zrow[:, 1:]], axis=1).reshape(ms, d)
    z = _conv_gate(u, b_gate, conv_w[0], s1, s2, m_prompt=m_prompt, seq=seq, dec_seq=dec_seq)
    tn = _pick(d, (512, 256, 128))
    (x1,) = _matmul(
        z, conv_w_out[0], col_offsets=(0,), n_cols=d, tm=tm, tn=tn, tk=tk, epilogue=_epi_residual,
        extras=(x0,), extra_specs=(_ij_spec(tm, tn),), out_shapes=[jax.ShapeDtypeStruct((m, d), F32)],
        out_specs=[_ij_spec(tm, tn)], name="conv_out")
    u_p = u[:m_prompt].reshape(batch, seq, d)
    conv_prompt = u_p[:, seq - (CONV_W - 1):][None]
    conv_sample = u[m_prompt:].reshape(dec_batch, dec_seq, d)[:, dec_seq - (CONV_W - 1):][None]

    (h1,) = _rmsnorm(x1, norm_ffn[0:1])
    tnf = _pick(d_ff, (512, 256, 128))
    (act,) = _matmul(
        h1, ffn_w_gu[0], col_offsets=(0, d_ff), n_cols=d_ff, tm=tm, tn=tnf, tk=tk, epilogue=_epi_swiglu,
        out_shapes=[jax.ShapeDtypeStruct((m, d_ff), BF16)], out_specs=[_ij_spec(tm, tnf)], name="ffn_gu")
    tkf = _pick(d_ff, (1024, 512, 256, 128))
    (x2,) = _matmul(
        act, ffn_w_down[0], col_offsets=(0,), n_cols=d, tm=tm, tn=tn, tk=tkf, epilogue=_epi_residual,
        extras=(x1,), extra_specs=(_ij_spec(tm, tn),), out_shapes=[jax.ShapeDtypeStruct((m, d), F32)],
        out_specs=[_ij_spec(tm, tn)], name="ffn_down")

    hkv, h2 = _rmsnorm(x2, jnp.stack([kv_norm, norm_mix[1]]))
    ones = jnp.ones((HEAD_DIM,), F32)
    kv_gain = jnp.stack([ones, ones, k_norm[1], ones, k_norm[2], ones]).reshape(2 * N_BRANCH, 1, HEAD_DIM)
    kv_f32, kv_bf = _matmul(
        hkv, w_kv, col_offsets=(0,), n_cols=2 * N_BRANCH * sec_w, tm=tm, tn=sec_w, tk=tk, epilogue=_epi_kv,
        extras=(kv_gain,), extra_specs=(pl.BlockSpec((None, 1, HEAD_DIM), lambda i, j, k: (j, 0, 0)),),
        out_shapes=[jax.ShapeDtypeStruct((m, 2 * N_BRANCH * sec_w), F32),
                    jax.ShapeDtypeStruct((m, 2 * N_BRANCH * sec_w), BF16)],
        out_specs=[_ij_spec(tm, sec_w)] * 2, name="kv_proj")
    kv_p = kv_f32[:m_prompt].reshape(batch, seq, 2 * N_BRANCH, N_KV, HEAD_DIM)
    kv_s = kv_f32[m_prompt:].reshape(dec_batch, dec_seq, 2 * N_BRANCH, N_KV, HEAD_DIM)
    kv_prompt = kv_p[:, :, :4]
    win_prompt = kv_p[:, seq - min(WINDOW, seq):, 4:]
    kv_sample = kv_s[:, :, :4]
    win_sample = jnp.concatenate([cache_win, kv_s[:, :, 4:]], axis=1)[:, dec_seq:]

    wq = w_qg[0]
    (q,) = _matmul(
        h2, wq, col_offsets=(0,), n_cols=d, tm=tm, tn=sec_w, tk=tk, epilogue=_epi_q,
        extras=(q_norm[0:1],), extra_specs=(pl.BlockSpec((1, HEAD_DIM), lambda i, j, k: (0, 0)),),
        out_shapes=[jax.ShapeDtypeStruct((m, d), BF16)], out_specs=[_ij_spec(tm, sec_w)], name="q_proj")
    wg = wq[:, d:].reshape(d, N_KV, hpg, N_BRANCH).transpose(0, 1, 3, 2).reshape(d, N_KV, N_BRANCH * hpg)
    wg = jnp.pad(wg, ((0, 0), (0, 0), (0, LANES - N_BRANCH * hpg))).reshape(d, N_KV * LANES)
    (gates,) = _matmul(
        h2, wg, col_offsets=(0,), n_cols=N_KV * LANES, tm=tm, tn=N_KV * LANES, tk=tk, epilogue=_epi_gate,
        out_shapes=[jax.ShapeDtypeStruct((m, N_KV * LANES), F32)], out_specs=[_ij_spec(tm, N_KV * LANES)],
        name="gate_proj")

    w1r = cmp_w1.reshape(2, R_CMP, STRIDE * HEAD_DIM, HEAD_DIM)
    per = cmp_pe.reshape(2, R_CMP, 1, STRIDE * HEAD_DIM)
    kn0 = k_norm[0:1]
    kc_p = _compress_prompt(kv_f32, w1r, per, cmp_w2, kn0, batch=batch, seq=seq)
    pt_flat = page_table.reshape(-1)
    kc_s = _compress_sample(cache_kv, pt_flat, w1r, per, cmp_w2, kn0, dec_batch=dec_batch, n_pages=n_pages,
                            page_size=page_size)

    o = _nsa_prompt(q, gates, kc_p, kv_bf, m_total=m, batch=batch, seq=seq, hpg=hpg)
    win2d = cache_win.reshape(dec_batch * l_win, 2 * sec_w)
    o = _nsa_sample(q, gates, kc_s, kv_f32, cache_kv, win2d, pt_flat, o, m_prompt=m_prompt,
                    dec_batch=dec_batch, dec_seq=dec_seq, hpg=hpg, past_len=past_len, n_pages=n_pages,
                    page_size=page_size, l_win=l_win)
    (x3,) = _matmul(
        o, w_o[0], col_offsets=(0,), n_cols=d, tm=tm, tn=tn, tk=tk, epilogue=_epi_residual,
        extras=(x2,), extra_specs=(_ij_spec(tm, tn),), out_shapes=[jax.ShapeDtypeStruct((m, d), F32)],
        out_specs=[_ij_spec(tm, tn)], name="attn_out")

    router_p = jnp.pad(moe_router[0], ((0, 0), (0, LANES - n_experts)))
    hp, ridx, wts = _router(x3, norm_ffn[1:2], router_p, n_experts)
    wgu = moe_w_gu[0].reshape(n_experts * d, 2 * d_ff)
    wdn = moe_w_down[0].reshape(n_experts * d_ff, d)
    sub = MOE_SUB_ROWS
    rc = -(-(TOP_K * m * 11 // (10 * n_experts)) // sub) * sub
    nch = TOP_K * m // rc + n_experts
    pos, nsub, cx, ce = _moe_plan(ridx[:, :TOP_K], n_experts, rc, nch, sub)
    xs = _moe_scatter(hp, pos, nch * rc)
    y = _moe_experts(xs, wgu, wdn, nsub, cx, ce, d=d, d_ff=d_ff, rc=rc, nch=nch, sub=sub)
    x4 = _moe_combine(x3, wts, y, pos)

    y_prompt = x4[:m_prompt].reshape(batch, seq, d)
    y_sample = x4[m_prompt:].reshape(dec_batch, dec_seq, d)
    return (y_prompt, y_sample, conv_prompt, kv_prompt, win_prompt, conv_sample, kv_sample, win_sample)
```

```python
import functools

import jax
import jax.numpy as jnp
from jax import lax
from jax.experimental import pallas as pl
from jax.experimental.pallas import tpu as pltpu

HEAD_DIM = 128
N_KV = 4
N_BRANCH = 3
L_CMP = 32
STRIDE = 16
R_CMP = L_CMP // STRIDE
L_SLC = 64
N_SEL = 16
WINDOW = 512
TOP_K = 2
CONV_W = 3
EPS = 1e-6
NEG = -1e30
FORCE = 1e6

LANES = 128
MOE_SUB_ROWS = 128
VMEM_LIMIT = 56 * 1024 * 1024

F32 = jnp.float32
BF16 = jnp.bfloat16


def _pick(n, candidates):
    for c in candidates:
        if c <= n and n % c == 0:
            return c
    return n


def _params(sem):
    return pltpu.CompilerParams(dimension_semantics=sem, vmem_limit_bytes=VMEM_LIMIT)


def _sigmoid(x):
    return 1.0 / (1.0 + jnp.exp(-x))


def _div_pow2(x, n):
    assert n & (n - 1) == 0
    return jnp.right_shift(x, n.bit_length() - 1)


def _rmsnorm_kernel(x_ref, g_ref, *o_refs):
    x = x_ref[...]
    y = x * lax.rsqrt(jnp.mean(x * x, axis=-1, keepdims=True) + EPS)
    for n, o_ref in enumerate(o_refs):
        o_ref[...] = (y * g_ref[n:n + 1, :]).astype(o_ref.dtype)


def _rmsnorm(x, gains):
    m, d = x.shape
    n = gains.shape[0]
    tm = _pick(m, (192, 128, 64, 32, 16))
    outs = pl.pallas_call(
        _rmsnorm_kernel,
        out_shape=[jax.ShapeDtypeStruct((m, d), BF16)] * n,
        grid=(m // tm,),
        in_specs=[pl.BlockSpec((tm, d), lambda i: (i, 0)),
                  pl.BlockSpec((n, d), lambda i: (0, 0))],
        out_specs=[pl.BlockSpec((tm, d), lambda i: (i, 0))] * n,
        compiler_params=_params(("parallel",)),
        name="rmsnorm",
    )(x, gains)
    return outs


def _mm_kernel(*refs, nk, n_w, n_extra, n_out, epilogue):
    x_ref = refs[0]
    w_refs = refs[1:1 + n_w]
    extra = refs[1 + n_w:1 + n_w + n_extra]
    out_refs = refs[1 + n_w + n_extra:1 + n_w + n_extra + n_out]
    acc_refs = refs[1 + n_w + n_extra + n_out:]
    i = pl.program_id(0)
    j = pl.program_id(1)
    k = pl.program_id(2)

    @pl.when(k == 0)
    def _():
        for a in acc_refs:
            a[...] = jnp.zeros_like(a)

    x = x_ref[...]
    for w_ref, a in zip(w_refs, acc_refs):
        a[...] += jnp.dot(x, w_ref[...].astype(BF16), preferred_element_type=F32)

    @pl.when(k == nk - 1)
    def _():
        epilogue((i, j), [a[...] for a in acc_refs], extra, out_refs)


def _matmul(x, w, *, col_offsets, n_cols, tm, tn, tk, epilogue, extras=(), extra_specs=(),
            out_shapes, out_specs, w_row_offset=0, name):
    m, kdim = x.shape
    nk = kdim // tk
    n_w = len(col_offsets)
    in_specs = [pl.BlockSpec((tm, tk), lambda i, j, k: (i, k))]
    for off in col_offsets:
        in_specs.append(pl.BlockSpec(
            (tk, tn), lambda i, j, k, off=off: (k + w_row_offset // tk, j + off // tn)))
    in_specs += list(extra_specs)
    kern = functools.partial(_mm_kernel, nk=nk, n_w=n_w, n_extra=len(extras),
                             n_out=len(out_shapes), epilogue=epilogue)
    return pl.pallas_call(
        kern,
        out_shape=out_shapes,
        grid=(m // tm, n_cols // tn, nk),
        in_specs=in_specs,
        out_specs=out_specs,
        scratch_shapes=[pltpu.VMEM((tm, tn), F32)] * n_w,
        compiler_params=_params(("parallel", "parallel", "arbitrary")),
        name=name,
    )(x, *([w] * n_w), *extras)


def _row_tile(m):
    return _pick(m, (2064, 2048, 1024, 688, 512, 256, 192, 128, 64, 32, 16))


def _ij_spec(tm, tn):
    return pl.BlockSpec((tm, tn), lambda i, j, k: (i, j))


def _epi_conv_in(ids, accs, extra, outs):
    b, c, v = accs
    outs[0][...] = b
    outs[1][...] = c * v


def _epi_residual(ids, accs, extra, outs):
    outs[0][...] = extra[0][...] + accs[0]


def _epi_swiglu(ids, accs, extra, outs):
    g, u = accs
    outs[0][...] = (g * _sigmoid(g) * u).astype(outs[0].dtype)


def _head_rmsnorm(a, gain):
    parts = []
    for h in range(a.shape[1] // HEAD_DIM):
        ah = a[:, h * HEAD_DIM:(h + 1) * HEAD_DIM]
        ms = jnp.mean(ah * ah, axis=-1, keepdims=True)
        parts.append(ah * lax.rsqrt(ms + EPS) * gain)
    return jnp.concatenate(parts, axis=1)


def _epi_kv(ids, accs, extra, outs):
    _, j = ids
    a = accs[0]
    normed = _head_rmsnorm(a, extra[0][...])
    y = jnp.where((j == 2) | (j == 4), normed, a)
    outs[0][...] = y
    outs[1][...] = y.astype(BF16)


def _epi_q(ids, accs, extra, outs):
    outs[0][...] = _head_rmsnorm(accs[0], extra[0][...]).astype(BF16)


def _epi_gate(ids, accs, extra, outs):
    outs[0][...] = _sigmoid(accs[0])


def _conv_kernel(u_ref, up_ref, b_ref, cw_ref, s1_ref, s2_ref, z_ref, *, tm, m_prompt, seq, dec_seq,
                 n_row_tiles):
    i = pl.program_id(0)
    u = u_ref[...]
    prev = up_ref[...]
    loc = lax.broadcasted_iota(jnp.int32, (tm, 1), 0)
    r = i * tm + loc
    u1 = pltpu.roll(u, 1, 0)
    u1 = jnp.where(loc == 0, prev[7:8, :], u1)
    u2 = pltpu.roll(u, 2, 0)
    u2 = jnp.where(loc == 0, prev[6:7, :], jnp.where(loc == 1, prev[7:8, :], u2))
    t = jnp.where(r < m_prompt, r & (seq - 1), (r - m_prompt) & (dec_seq - 1))
    u1 = jnp.where(t >= 1, u1, 0.0)
    u2 = jnp.where(t >= 2, u2, 0.0)
    w0 = cw_ref[0:1, :]
    w1 = cw_ref[1:2, :]
    w2 = cw_ref[2:3, :]
    conv = w2 * u + w1 * u1 + w0 * u2
    z_ref[...] = (b_ref[...] * conv).astype(z_ref.dtype)
    ms = s1_ref.shape[0]

    @pl.when(i == n_row_tiles - 1)
    def _():
        tail = conv[tm - ms:, :] + w1 * s1_ref[...] + w0 * s2_ref[...]
        z_ref[tm - ms:, :] = (b_ref[tm - ms:, :] * tail).astype(z_ref.dtype)


def _conv_gate(u, b, cw, s1, s2, *, m_prompt, seq, dec_seq):
    m, d = u.shape
    ms = s1.shape[0]
    tm = _row_tile(m)
    tc = _pick(d, (512, 256, 128))
    n_row_tiles = m // tm
    kern = functools.partial(_conv_kernel, tm=tm, m_prompt=m_prompt, seq=seq, dec_seq=dec_seq,
                             n_row_tiles=n_row_tiles)
    return pl.pallas_call(
        kern,
        out_shape=jax.ShapeDtypeStruct((m, d), BF16),
        grid=(n_row_tiles, d // tc),
        in_specs=[pl.BlockSpec((tm, tc), lambda i, j: (i, j)),
                  pl.BlockSpec((8, tc), lambda i, j: (jnp.maximum(i * (tm // 8) - 1, 0), j)),
                  pl.BlockSpec((tm, tc), lambda i, j: (i, j)),
                  pl.BlockSpec((CONV_W, tc), lambda i, j: (0, j)),
                  pl.BlockSpec((ms, tc), lambda i, j: (0, j)),
                  pl.BlockSpec((ms, tc), lambda i, j: (0, j))],
        out_specs=pl.BlockSpec((tm, tc), lambda i, j: (i, j)),
        compiler_params=_params(("parallel", "parallel")),
        name="conv_gate",
    )(u, u, b, cw, s1, s2)


def _cmp_stage1_compute(head_refs, w1_ref, pe_ref, top_ref, bot_ref, nch):
    for sec in range(2):
        wt = w1_ref[sec, 0].astype(BF16)
        wb = w1_ref[sec, 1].astype(BF16)
        pt = pe_ref[sec, 0]
        pb = pe_ref[sec, 1]
        rows = []
        for g in range(N_KV):
            ref = head_refs[sec * N_KV + g]
            cols = [ref[pl.ds(s, nch, stride=STRIDE), :] for s in range(STRIDE)]
            rows.append(jnp.concatenate(cols, axis=1))
        a = jnp.concatenate(rows, axis=0)
        top = jnp.dot((a + pt).astype(BF16), wt, preferred_element_type=F32)
        bot = jnp.dot((a + pb).astype(BF16), wb, preferred_element_type=F32)
        for g in range(N_KV):
            hh = sec * N_KV + g
            top_ref[:, hh * HEAD_DIM:(hh + 1) * HEAD_DIM] = top[g * nch:(g + 1) * nch, :]
            bot_ref[:, hh * HEAD_DIM:(hh + 1) * HEAD_DIM] = bot[g * nch:(g + 1) * nch, :]


def _cmp1_prompt_kernel(*refs, nch):
    n_heads = 2 * N_KV
    head_refs = refs[:n_heads]
    w1_ref, pe_ref, top_ref, bot_ref = refs[n_heads:]
    _cmp_stage1_compute(head_refs, w1_ref, pe_ref, top_ref, bot_ref, nch)


def _cmp1_sample_kernel(pt_ref, cache_ref, w1_ref, pe_ref, top_ref, bot_ref, buf_ref, sem, *,
                        pages_per_step, n_pages, page_size):
    b = pl.program_id(0)
    grp = pl.program_id(1)
    n_heads = 2 * N_KV

    def copy(p, hh):
        page = pt_ref[b * n_pages + grp * pages_per_step + p]
        return pltpu.make_async_copy(
            cache_ref.at[page, :, hh // N_KV, hh % N_KV, :],
            buf_ref.at[hh, pl.ds(p * page_size, page_size), :],
            sem.at[0])

    for p in range(pages_per_step):
        for hh in range(n_heads):
            copy(p, hh).start()
    for p in range(pages_per_step):
        for hh in range(n_heads):
            copy(p, hh).wait()
    _cmp_stage1_compute([buf_ref.at[hh] for hh in range(n_heads)], w1_ref, pe_ref, top_ref, bot_ref,
                        pages_per_step * page_size // STRIDE)


def _cmp2_kernel(top_ref, bot_ref, w2_ref, kn_ref, o_ref, *, nch):
    pre = top_ref[...] + pltpu.roll(bot_ref[...], nch - 1, 0)
    a = (pre * _sigmoid(pre)).astype(BF16)
    for sec in range(2):
        w2 = w2_ref[sec].astype(BF16)
        for g in range(N_KV):
            hh = sec * N_KV + g
            y = jnp.dot(a[:, hh * HEAD_DIM:(hh + 1) * HEAD_DIM], w2, preferred_element_type=F32)
            if sec == 0:
                y = _head_rmsnorm(y, kn_ref[...])
            o_ref[:, hh * HEAD_DIM:(hh + 1) * HEAD_DIM] = y.astype(o_ref.dtype)


def _cmp_stage2(top, bot, w2, kn0, nch):
    rows, width = top.shape
    return pl.pallas_call(
        functools.partial(_cmp2_kernel, nch=nch),
        out_shape=jax.ShapeDtypeStruct((rows, width), BF16),
        grid=(rows // nch,),
        in_specs=[pl.BlockSpec((nch, width), lambda b: (b, 0)),
                  pl.BlockSpec((nch, width), lambda b: (b, 0)),
                  pl.BlockSpec(w2.shape, lambda b: (0, 0, 0)),
                  pl.BlockSpec(kn0.shape, lambda b: (0, 0))],
        out_specs=pl.BlockSpec((nch, width), lambda b: (b, 0)),
        compiler_params=_params(("parallel",)),
        name="cmp_stage2",
    )(top, bot, w2, kn0)


def _compress_prompt(kv_f32, w1r, per, w2, kn0, *, batch, seq):
    width = 2 * N_KV * HEAD_DIM
    nch = seq // STRIDE
    top, bot = pl.pallas_call(
        functools.partial(_cmp1_prompt_kernel, nch=nch),
        out_shape=[jax.ShapeDtypeStruct((batch * nch, width), F32)] * 2,
        grid=(batch,),
        in_specs=[pl.BlockSpec((seq, HEAD_DIM), lambda b, hh=hh: (b, hh)) for hh in range(2 * N_KV)]
        + [pl.BlockSpec(w1r.shape, lambda b: (0, 0, 0, 0)),
           pl.BlockSpec(per.shape, lambda b: (0, 0, 0, 0))],
        out_specs=[pl.BlockSpec((nch, width), lambda b: (b, 0))] * 2,
        compiler_params=_params(("parallel",)),
        name="cmp_stage1_prompt",
    )(*([kv_f32] * (2 * N_KV)), w1r, per)
    return _cmp_stage2(top, bot, w2, kn0, nch)


def _compress_sample(cache2d, pt_flat, w1r, per, w2, kn0, *, dec_batch, n_pages, page_size):
    width = 2 * N_KV * HEAD_DIM
    pages_per_step = _pick(n_pages, (16, 8, 4, 2, 1))
    n_groups = n_pages // pages_per_step
    rows_step = pages_per_step * page_size
    nch_step = rows_step // STRIDE
    nch = n_pages * page_size // STRIDE
    kern = functools.partial(_cmp1_sample_kernel, pages_per_step=pages_per_step, n_pages=n_pages,
                             page_size=page_size)
    top, bot = pl.pallas_call(
        kern,
        out_shape=[jax.ShapeDtypeStruct((dec_batch * nch, width), F32)] * 2,
        grid_spec=pltpu.PrefetchScalarGridSpec(
            num_scalar_prefetch=1,
            grid=(dec_batch, n_groups),
            in_specs=[pl.BlockSpec(memory_space=pl.ANY),
                      pl.BlockSpec(w1r.shape, lambda b, g, pt: (0, 0, 0, 0)),
                      pl.BlockSpec(per.shape, lambda b, g, pt: (0, 0, 0, 0))],
            out_specs=[pl.BlockSpec((nch_step, width), lambda b, g, pt: (b * n_groups + g, 0))] * 2,
            scratch_shapes=[pltpu.VMEM((2 * N_KV, rows_step, HEAD_DIM), F32),
                            pltpu.SemaphoreType.DMA((1,))]),
        compiler_params=_params(("arbitrary", "arbitrary")),
        name="cmp_stage1_sample",
    )(pt_flat, cache2d, w1r, per)
    return _cmp_stage2(top, bot, w2, kn0, nch)


def _split3(x):
    hi = x.astype(BF16)
    r1 = x - hi.astype(F32)
    mid = r1.astype(BF16)
    lo = (r1 - mid.astype(F32)).astype(BF16)
    return hi, mid, lo


def _dot_f32ish(x, m_bf16):
    hi, mid, lo = _split3(x)
    out = jnp.dot(lo, m_bf16, preferred_element_type=F32)
    out = out + jnp.dot(mid, m_bf16, preferred_element_type=F32)
    return out + jnp.dot(hi, m_bf16, preferred_element_type=F32)


def _inter_t(n_c, n_s_lanes):
    c0 = lax.broadcasted_iota(jnp.int32, (n_c, n_s_lanes), 0) * STRIDE
    s0 = lax.broadcasted_iota(jnp.int32, (n_c, n_s_lanes), 1) * L_SLC
    return ((c0 < s0 + L_SLC) & (c0 + L_CMP > s0)).astype(BF16)


def _rank_desc(score, ns):
    lane = lax.broadcasted_iota(jnp.int32, score.shape, 1)
    rank = jnp.zeros(score.shape, jnp.int32)
    for sp in range(ns):
        col = score[:, sp:sp + 1]
        ahead = (col > score) | ((col == score) & (lane > sp))
        rank = rank + ahead.astype(jnp.int32)
    return rank


def _selection_scores(imp, qblk, ns):
    j = lax.broadcasted_iota(jnp.int32, imp.shape, 1)
    forced = (j == 0) | (j == qblk) | (j == qblk - 1)
    score = jnp.where(forced, FORCE, imp)
    score = jnp.where(j <= qblk, score, -FORCE)
    return jnp.where(j < ns, score, -2.0 * FORCE)


def _nsa_prompt_kernel(q_ref, g_ref, kc_ref, vc_ref, ks_ref, vs_ref, kw_ref, vw_ref, o_init_ref, o_ref,
                       m_sc, l_sc, acc_sc, *, tq, tk, hpg, seq, ncp):
    del o_init_ref
    qt = pl.program_id(2)
    t0 = qt * tq
    scale = HEAD_DIM ** -0.5
    rows = hpg * tq
    qf = q_ref[...]
    q = jnp.concatenate([qf[:, h * HEAD_DIM:(h + 1) * HEAD_DIM] for h in range(hpg)], axis=0)
    qpos = t0 + lax.broadcasted_iota(jnp.int32, (tq, 1), 0)
    nt = (((1,), (1,)), ((), ()))

    s = lax.dot_general(q, kc_ref[...], nt, preferred_element_type=F32) * scale
    c_end = lax.broadcasted_iota(jnp.int32, (tq, ncp), 1) * STRIDE + (L_CMP - 1)
    mc = (c_end <= qpos)[None]
    s3 = jnp.where(mc, s.reshape(hpg, tq, ncp), NEG)
    e = jnp.where(mc, jnp.exp(s3 - jnp.max(s3, axis=-1, keepdims=True)), 0.0)
    den = jnp.sum(e, axis=-1, keepdims=True)
    p_c = jnp.where(den > 0.0, e / jnp.where(den > 0.0, den, 1.0), 0.0)
    o_c = jnp.dot(p_c.reshape(rows, ncp).astype(BF16), vc_ref[...], preferred_element_type=F32)
    psum = jnp.sum(p_c, axis=0)

    ns = -(-seq // L_SLC)
    imp = _dot_f32ish(psum, _inter_t(ncp, LANES))
    qblk = _div_pow2(qpos, L_SLC)
    score = _selection_scores(imp, qblk, ns)
    rank = _rank_desc(score, ns)
    sel = ((rank < N_SEL) & (score > -0.5 * FORCE)).astype(BF16)

    m_sc[...] = jnp.full(m_sc.shape, NEG, F32)
    l_sc[...] = jnp.zeros(l_sc.shape, F32)
    acc_sc[...] = jnp.zeros(acc_sc.shape, F32)

    def body(j, carry):
        k0 = pl.multiple_of(j * tk, tk)
        kt = ks_ref[pl.ds(k0, tk), :]
        vt = vs_ref[pl.ds(k0, tk), :]
        sj = lax.dot_general(q, kt, nt, preferred_element_type=F32) * scale
        kpos = k0 + lax.broadcasted_iota(jnp.int32, (1, tk), 1)
        expand = (lax.broadcasted_iota(jnp.int32, (LANES, tk), 0) == _div_pow2(kpos, L_SLC)).astype(BF16)
        km = jnp.dot(sel, expand, preferred_element_type=F32) > 0.5
        mask = (km & (kpos <= qpos))[None]
        sj3 = jnp.where(mask, sj.reshape(hpg, tq, tk), NEG)
        m_old = m_sc[...]
        m_new = jnp.maximum(m_old, jnp.max(sj3, axis=-1, keepdims=True))
        p = jnp.where(mask, jnp.exp(sj3 - m_new), 0.0)
        alpha = jnp.exp(m_old - m_new)
        l_sc[...] = alpha * l_sc[...] + jnp.sum(p, axis=-1, keepdims=True)
        pv = jnp.dot(p.reshape(rows, tk).astype(BF16), vt, preferred_element_type=F32)
        acc_sc[...] = alpha * acc_sc[...] + pv.reshape(hpg, tq, HEAD_DIM)
        m_sc[...] = m_new
        return carry

    lax.fori_loop(0, (t0 + tq - 1) // tk + 1, body, 0)
    o_s = acc_sc[...] / l_sc[...]

    wl = WINDOW + tq
    w0 = pl.multiple_of(jnp.clip(t0 - WINDOW, 0, seq - wl), tq)
    kwt = kw_ref[pl.ds(w0, wl), :]
    vwt = vw_ref[pl.ds(w0, wl), :]
    sw = lax.dot_general(q, kwt, nt, preferred_element_type=F32) * scale
    wpos = w0 + lax.broadcasted_iota(jnp.int32, (1, wl), 1)
    dlt = qpos - wpos
    mw = ((dlt >= 0) & (dlt <= WINDOW))[None]
    sw3 = jnp.where(mw, sw.reshape(hpg, tq, wl), NEG)
    ew = jnp.where(mw, jnp.exp(sw3 - jnp.max(sw3, axis=-1, keepdims=True)), 0.0)
    lw = jnp.sum(ew, axis=-1, keepdims=True)
    o_w = jnp.dot(ew.reshape(rows, wl).astype(BF16), vwt, preferred_element_type=F32)
    o_w = o_w.reshape(hpg, tq, HEAD_DIM) / lw

    gt = g_ref[...]
    o_c3 = o_c.reshape(hpg, tq, HEAD_DIM)
    for h in range(hpg):
        g0 = gt[:, 0 * hpg + h:0 * hpg + h + 1]
        g1 = gt[:, 1 * hpg + h:1 * hpg + h + 1]
        g2 = gt[:, 2 * hpg + h:2 * hpg + h + 1]
        o_ref[:, h * HEAD_DIM:(h + 1) * HEAD_DIM] = (
            g0 * o_c3[h] + g1 * o_s[h] + g2 * o_w[h]).astype(o_ref.dtype)


def _nsa_prompt(q, gates, kc, kv_bf, *, m_total, batch, seq, hpg):
    d = q.shape[1]
    tq = 128
    tk = _pick(seq, (512, 256, 128))
    ncp = seq // STRIDE
    nqt = seq // tq
    gw = hpg * HEAD_DIM
    rows = hpg * tq
    kern = functools.partial(_nsa_prompt_kernel, tq=tq, tk=tk, hpg=hpg, seq=seq, ncp=ncp)

    def kvspec(sec):
        return pl.BlockSpec((seq, HEAD_DIM), lambda b, g, t, sec=sec: (b, sec * N_KV + g))

    return pl.pallas_call(
        kern,
        out_shape=jax.ShapeDtypeStruct((m_total, d), BF16),
        grid=(batch, N_KV, nqt),
        in_specs=[pl.BlockSpec((tq, gw), lambda b, g, t: (b * nqt + t, g)),
                  pl.BlockSpec((tq, LANES), lambda b, g, t: (b * nqt + t, g)),
                  pl.BlockSpec((ncp, HEAD_DIM), lambda b, g, t: (b, g)),
                  pl.BlockSpec((ncp, HEAD_DIM), lambda b, g, t: (b, N_KV + g)),
                  kvspec(2), kvspec(3), kvspec(4), kvspec(5),
                  pl.BlockSpec(memory_space=pl.ANY)],
        out_specs=pl.BlockSpec((tq, gw), lambda b, g, t: (b * nqt + t, g)),
        scratch_shapes=[pltpu.VMEM((hpg, tq, 1), F32), pltpu.VMEM((hpg, tq, 1), F32),
                        pltpu.VMEM((hpg, tq, HEAD_DIM), F32)],
        input_output_aliases={8: 0},
        compiler_params=_params(("parallel", "parallel", "arbitrary")),
        name="nsa_prompt",
    )(q, gates, kc, kc, kv_bf, kv_bf, kv_bf, kv_bf, jnp.zeros((m_total, d), BF16))


def _nsa_sample_select_kernel(q_ref, kc_ref, vc_ref, oc_ref, idx_ref, *, dec_batch, dec_seq, hpg,
                              past_len, ncp, ns, ns_lanes):
    scale = HEAD_DIM ** -0.5
    nt = (((1,), (1,)), ((), ()))
    qf = q_ref[...].astype(F32)
    rows = hpg * dec_seq
    qrow = lax.broadcasted_iota(jnp.int32, (dec_seq, 1), 0)
    qpos = past_len + qrow
    c_end = lax.broadcasted_iota(jnp.int32, (dec_seq, ncp), 1) * STRIDE + (L_CMP - 1)
    mc = (c_end <= qpos)[None]
    psums = []
    for b in range(dec_batch):
        qb = qf[b * dec_seq:(b + 1) * dec_seq, :]
        q = jnp.concatenate([qb[:, h * HEAD_DIM:(h + 1) * HEAD_DIM] for h in range(hpg)],
                            axis=0).astype(BF16)
        kc = kc_ref[b * ncp:(b + 1) * ncp, :]
        vc = vc_ref[b * ncp:(b + 1) * ncp, :]
        s = lax.dot_general(q, kc, nt, preferred_element_type=F32) * scale
        s3 = jnp.where(mc, s.reshape(hpg, dec_seq, ncp), NEG)
        e = jnp.where(mc, jnp.exp(s3 - jnp.max(s3, axis=-1, keepdims=True)), 0.0)
        den = jnp.sum(e, axis=-1, keepdims=True)
        p_c = jnp.where(den > 0.0, e / jnp.where(den > 0.0, den, 1.0), 0.0)
        o_c = jnp.dot(p_c.reshape(rows, ncp).astype(BF16), vc, preferred_element_type=F32)
        for h in range(hpg):
            oc_ref[b * dec_seq:(b + 1) * dec_seq, h * HEAD_DIM:(h + 1) * HEAD_DIM] = (
                o_c[h * dec_seq:(h + 1) * dec_seq, :])
        psums.append(jnp.sum(p_c, axis=0))
    psum = jnp.concatenate(psums, axis=0)
    imp = _dot_f32ish(psum, _inter_t(ncp, ns_lanes))
    n_rows = dec_batch * dec_seq
    qpos_all = past_len + (lax.broadcasted_iota(jnp.int32, (n_rows, 1), 0) & (dec_seq - 1))
    score = _selection_scores(imp, _div_pow2(qpos_all, L_SLC), ns)
    rank = _rank_desc(score, ns)
    ok = score > -0.5 * FORCE
    lane = lax.broadcasted_iota(jnp.int32, score.shape, 1)
    out_lane = lax.broadcasted_iota(jnp.int32, (n_rows, LANES), 1)
    idx = jnp.full((n_rows, LANES), -1, jnp.int32)
    for k in range(N_SEL):
        hit = (rank == k) & ok
        blk = jnp.sum(jnp.where(hit, (lane + 1).astype(F32), 0.0), axis=-1, keepdims=True)
        blk = blk.astype(jnp.int32) - 1
        idx = jnp.where(out_lane == k, blk, idx)
    idx_ref[...] = idx


def _nsa_sample_attend_kernel(idx_ref, pt_ref, q_ref, g_ref, oc_ref, knew_ref, vnew_ref, kwo_ref, vwo_ref,
                              kwn_ref, vwn_ref, cache_ref, o_in_ref, o_ref, kbuf, vbuf, of32, sems, *,
                              dec_batch, dec_seq, hpg, past_len, n_pages, page_size, l_win):
    del o_in_ref
    g = pl.program_id(0)
    scale = HEAD_DIM ** -0.5
    nt = (((1,), (1,)), ((), ()))
    rows = hpg * dec_seq
    n_past_blocks = past_len // L_SLC
    blocks_per_page = page_size // L_SLC
    nkeys = N_SEL * L_SLC
    qf = q_ref[...].astype(F32)
    gt = g_ref[...]
    qrow = lax.broadcasted_iota(jnp.int32, (rows, 1), 0) & (dec_seq - 1)
    key_slot = _div_pow2(lax.broadcasted_iota(jnp.int32, (1, nkeys), 1), L_SLC)
    newj = lax.broadcasted_iota(jnp.int32, (1, dec_seq), 1)

    for b in range(dec_batch):
        def copies(qi, k):
            blk = idx_ref[((g * dec_batch + b) * dec_seq + qi) * N_SEL + k]
            blk = jnp.clip(blk, 0, n_past_blocks - 1)
            page = pt_ref[b * n_pages + blk // blocks_per_page]
            row0 = pl.multiple_of((blk % blocks_per_page) * L_SLC, L_SLC)
            ck = pltpu.make_async_copy(cache_ref.at[page, pl.ds(row0, L_SLC), 2, g, :],
                                       kbuf.at[qi, pl.ds(k * L_SLC, L_SLC), :], sems.at[0])
            cv = pltpu.make_async_copy(cache_ref.at[page, pl.ds(row0, L_SLC), 3, g, :],
                                       vbuf.at[qi, pl.ds(k * L_SLC, L_SLC), :], sems.at[1])
            return ck, cv

        for qi in range(dec_seq):
            for k in range(N_SEL):
                ck, cv = copies(qi, k)
                ck.start()
                cv.start()
        for qi in range(dec_seq):
            for k in range(N_SEL):
                ck, cv = copies(qi, k)
                ck.wait()
                cv.wait()

        qb = qf[b * dec_seq:(b + 1) * dec_seq, :]
        q = jnp.concatenate([qb[:, h * HEAD_DIM:(h + 1) * HEAD_DIM] for h in range(hpg)],
                            axis=0).astype(BF16)
        knew = knew_ref[b * dec_seq:(b + 1) * dec_seq, :].astype(BF16)
        vnew = vnew_ref[b * dec_seq:(b + 1) * dec_seq, :].astype(BF16)
        s_new = lax.dot_general(q, knew, nt, preferred_element_type=F32) * scale

        o_s = jnp.zeros((rows, HEAD_DIM), F32)
        for qi in range(dec_seq):
            valid = jnp.zeros((1, nkeys), jnp.int32)
            has_new = jnp.int32(0)
            for k in range(N_SEL):
                blk = idx_ref[((g * dec_batch + b) * dec_seq + qi) * N_SEL + k]
                is_past = ((blk >= 0) & (blk < n_past_blocks)).astype(jnp.int32)
                valid = jnp.where(key_slot == k, is_past, valid)
                has_new = has_new | (blk == n_past_blocks).astype(jnp.int32)
            mk = valid > 0
            mn = ((newj <= qi).astype(jnp.int32) * has_new) > 0
            kq = kbuf[qi].astype(BF16)
            vq = vbuf[qi].astype(BF16)
            s_old = lax.dot_general(q, kq, nt, preferred_element_type=F32) * scale
            s_old = jnp.where(mk, s_old, NEG)
            s_n = jnp.where(mn, s_new, NEG)
            mx = jnp.maximum(jnp.max(s_old, axis=-1, keepdims=True), jnp.max(s_n, axis=-1, keepdims=True))
            p_old = jnp.where(mk, jnp.exp(s_old - mx), 0.0)
            p_n = jnp.where(mn, jnp.exp(s_n - mx), 0.0)
            den = jnp.sum(p_old, axis=-1, keepdims=True) + jnp.sum(p_n, axis=-1, keepdims=True)
            o = (jnp.dot(p_old.astype(BF16), vq, preferred_element_type=F32)
                 + jnp.dot(p_n.astype(BF16), vnew, preferred_element_type=F32)) / den
            o_s = jnp.where(qrow == qi, o, o_s)

        kwo = kwo_ref[b * l_win:(b + 1) * l_win, :].astype(BF16)
        vwo = vwo_ref[b * l_win:(b + 1) * l_win, :].astype(BF16)
        kwn = kwn_ref[b * dec_seq:(b + 1) * dec_seq, :].astype(BF16)
        vwn = vwn_ref[b * dec_seq:(b + 1) * dec_seq, :].astype(BF16)
        qpos = past_len + qrow
        wpos_o = past_len - l_win + lax.broadcasted_iota(jnp.int32, (1, l_win), 1)
        wpos_n = past_len + newj
        d_o = qpos - wpos_o
        d_n = qpos - wpos_n
        m_o = (d_o >= 0) & (d_o <= WINDOW) & (wpos_o >= 0)
        m_n = (d_n >= 0) & (d_n <= WINDOW)
        sw_o = jnp.where(m_o, lax.dot_general(q, kwo, nt, preferred_element_type=F32) * scale, NEG)
        sw_n = jnp.where(m_n, lax.dot_general(q, kwn, nt, preferred_element_type=F32) * scale, NEG)
        mx = jnp.maximum(jnp.max(sw_o, axis=-1, keepdims=True), jnp.max(sw_n, axis=-1, keepdims=True))
        pw_o = jnp.where(m_o, jnp.exp(sw_o - mx), 0.0)
        pw_n = jnp.where(m_n, jnp.exp(sw_n - mx), 0.0)
        den = jnp.sum(pw_o, axis=-1, keepdims=True) + jnp.sum(pw_n, axis=-1, keepdims=True)
        o_w = (jnp.dot(pw_o.astype(BF16), vwo, preferred_element_type=F32)
               + jnp.dot(pw_n.astype(BF16), vwn, preferred_element_type=F32)) / den

        gb = gt[b * dec_seq:(b + 1) * dec_seq, :]
        for h in range(hpg):
            sl = slice(h * dec_seq, (h + 1) * dec_seq)
            g0 = gb[:, 0 * hpg + h:0 * hpg + h + 1]
            g1 = gb[:, 1 * hpg + h:1 * hpg + h + 1]
            g2 = gb[:, 2 * hpg + h:2 * hpg + h + 1]
            o_c = oc_ref[b * dec_seq:(b + 1) * dec_seq, h * HEAD_DIM:(h + 1) * HEAD_DIM]
            of32[b * dec_seq:(b + 1) * dec_seq, h * HEAD_DIM:(h + 1) * HEAD_DIM] = (
                g0 * o_c + g1 * o_s[sl, :] + g2 * o_w[sl, :])
    o_ref[...] = of32[...].astype(o_ref.dtype)


def _nsa_sample(q, gates, kc_s, kv_f32, cache2d, win2d, pt_flat, o_prompt, *, m_prompt, dec_batch,
                dec_seq, hpg, past_len, n_pages, page_size, l_win):
    ms = dec_batch * dec_seq
    gw = hpg * HEAD_DIM
    ncp = past_len // STRIDE
    t_all = past_len + dec_seq
    ns = -(-t_all // L_SLC)
    ns_lanes = -(-ns // LANES) * LANES
    rb = m_prompt // ms
    sel = functools.partial(_nsa_sample_select_kernel, dec_batch=dec_batch, dec_seq=dec_seq, hpg=hpg,
                            past_len=past_len, ncp=ncp, ns=ns, ns_lanes=ns_lanes)
    oc, idx = pl.pallas_call(
        sel,
        out_shape=[jax.ShapeDtypeStruct((ms, N_KV * gw), F32),
                   jax.ShapeDtypeStruct((N_KV * ms, LANES), jnp.int32)],
        grid=(N_KV,),
        in_specs=[pl.BlockSpec((ms, gw), lambda g: (rb, g)),
                  pl.BlockSpec((dec_batch * ncp, HEAD_DIM), lambda g: (0, g)),
                  pl.BlockSpec((dec_batch * ncp, HEAD_DIM), lambda g: (0, N_KV + g))],
        out_specs=[pl.BlockSpec((ms, gw), lambda g: (0, g)),
                   pl.BlockSpec((ms, LANES), lambda g: (g, 0))],
        compiler_params=_params(("parallel",)),
        name="nsa_sample_select",
    )(q, kc_s, kc_s)
    idx_flat = idx[:, :N_SEL].reshape(-1)

    att = functools.partial(_nsa_sample_attend_kernel, dec_batch=dec_batch, dec_seq=dec_seq, hpg=hpg,
                            past_len=past_len, n_pages=n_pages, page_size=page_size, l_win=l_win)

    def sm(f):
        return lambda g, idx_r, pt_r: f(g)

    return pl.pallas_call(
        att,
        out_shape=jax.ShapeDtypeStruct(o_prompt.shape, o_prompt.dtype),
        grid_spec=pltpu.PrefetchScalarGridSpec(
            num_scalar_prefetch=2,
            grid=(N_KV,),
            in_specs=[pl.BlockSpec((ms, gw), sm(lambda g: (rb, g))),
                      pl.BlockSpec((ms, LANES), sm(lambda g: (rb, g))),
                      pl.BlockSpec((ms, gw), sm(lambda g: (0, g))),
                      pl.BlockSpec((ms, HEAD_DIM), sm(lambda g: (rb, 2 * N_KV + g))),
                      pl.BlockSpec((ms, HEAD_DIM), sm(lambda g: (rb, 3 * N_KV + g))),
                      pl.BlockSpec((dec_batch * l_win, HEAD_DIM), sm(lambda g: (0, g))),
                      pl.BlockSpec((dec_batch * l_win, HEAD_DIM), sm(lambda g: (0, N_KV + g))),
                      pl.BlockSpec((ms, HEAD_DIM), sm(lambda g: (rb, 4 * N_KV + g))),
                      pl.BlockSpec((ms, HEAD_DIM), sm(lambda g: (rb, 5 * N_KV + g))),
                      pl.BlockSpec(memory_space=pl.ANY),
                      pl.BlockSpec(memory_space=pl.ANY)],
            out_specs=pl.BlockSpec((ms, gw), sm(lambda g: (rb, g))),
            scratch_shapes=[pltpu.VMEM((dec_seq, N_SEL * L_SLC, HEAD_DIM), F32),
                            pltpu.VMEM((dec_seq, N_SEL * L_SLC, HEAD_DIM), F32),
                            pltpu.VMEM((ms, gw), F32),
                            pltpu.SemaphoreType.DMA((2,))]),
        input_output_aliases={12: 0},
        compiler_params=_params(("arbitrary",)),
        name="nsa_sample_attend",
    )(idx_flat, pt_flat, q, gates, oc, kv_f32, kv_f32, win2d, win2d, kv_f32, kv_f32, cache2d, o_prompt)


def _router_kernel(x_ref, g_ref, r_ref, hp_ref, idx_ref, wts_ref, *, n_experts):
    x = x_ref[...]
    h = x * lax.rsqrt(jnp.mean(x * x, axis=-1, keepdims=True) + EPS) * g_ref[...]
    half = h.shape[1] // 2
    bits = pltpu.bitcast(h.astype(BF16).astype(F32), jnp.uint32)
    hp_ref[...] = (bits[:, :half] >> 16) | (bits[:, half:] & jnp.uint32(0xFFFF0000))
    r_hi, r_mid, r_lo = _split3(r_ref[...])
    h_hi, h_mid, h_lo = _split3(h)

    def d(a, b):
        return jnp.dot(a, b, preferred_element_type=F32)

    logits = (d(h_lo, r_hi) + d(h_hi, r_lo) + d(h_mid, r_mid)) + (d(h_mid, r_hi) + d(h_hi, r_mid)) + d(h_hi, r_hi)
    lane = lax.broadcasted_iota(jnp.int32, logits.shape, 1)
    logits = jnp.where(lane < n_experts, logits, -jnp.inf)
    v1 = jnp.max(logits, axis=-1, keepdims=True)
    i1 = jnp.min(jnp.where(logits == v1, lane, LANES), axis=-1, keepdims=True)
    rest = jnp.where(lane == i1, -jnp.inf, logits)
    v2 = jnp.max(rest, axis=-1, keepdims=True)
    i2 = jnp.min(jnp.where(rest == v2, lane, LANES), axis=-1, keepdims=True)
    e2 = jnp.exp(v2 - v1)
    w1 = 1.0 / (1.0 + e2)
    w2 = e2 / (1.0 + e2)
    idx_ref[...] = jnp.where(lane == 0, i1, jnp.where(lane == 1, i2, 0))
    wts_ref[...] = jnp.where(lane == 0, w1, jnp.where(lane == 1, w2, 0.0))


def _router(x, gain, router_p, n_experts):
    m, d = x.shape
    tm = _pick(m, (192, 128, 64, 32, 16))
    return pl.pallas_call(
        functools.partial(_router_kernel, n_experts=n_experts),
        out_shape=[jax.ShapeDtypeStruct((m, d // 2), jnp.uint32),
                   jax.ShapeDtypeStruct((m, LANES), jnp.int32),
                   jax.ShapeDtypeStruct((m, LANES), F32)],
        grid=(m // tm,),
        in_specs=[pl.BlockSpec((tm, d), lambda i: (i, 0)),
                  pl.BlockSpec((1, d), lambda i: (0, 0)),
                  pl.BlockSpec((d, LANES), lambda i: (0, 0))],
        out_specs=[pl.BlockSpec((tm, d // 2), lambda i: (i, 0)),
                   pl.BlockSpec((tm, LANES), lambda i: (i, 0)),
                   pl.BlockSpec((tm, LANES), lambda i: (i, 0))],
        compiler_params=_params(("parallel",)),
        name="moe_router",
    )(x, gain, router_p)


def _moe_plan(idx2, n_experts, rc, nch, sub):
    e = idx2.reshape(-1)
    onehot = (e[:, None] == jnp.arange(n_experts, dtype=jnp.int32)[None, :]).astype(jnp.int32)
    csum = jnp.cumsum(onehot, axis=0)
    rank = jnp.take_along_axis(csum, e[:, None], axis=1)[:, 0] - 1
    counts = csum[-1]
    nchunks = (counts + rc - 1) // rc
    cend = jnp.cumsum(nchunks)
    cstart = cend - nchunks
    pos = (cstart[e] + rank // rc) * rc + rank % rc
    n_active = cend[-1]
    c = jnp.arange(nch, dtype=jnp.int32)
    cx = jnp.minimum(c, n_active - 1)
    ce = jnp.sum((cend[None, :] <= cx[:, None]).astype(jnp.int32), axis=1)
    rows = jnp.clip(counts[ce] - (cx - cstart[ce]) * rc, 0, rc)
    nsub = jnp.where(c < n_active, (rows + sub - 1) // sub, 0)
    return pos.astype(jnp.int32), nsub.astype(jnp.int32), cx.astype(jnp.int32), ce


def _moe_scatter_kernel(pos_ref, hp_ref, xs_in_ref, xs_ref, sems, *, tt):
    del xs_in_ref

    def copies(r):
        return [pltpu.make_async_copy(hp_ref.at[pl.ds(r, 1), :],
                                      xs_ref.at[pl.ds(pos_ref[0, 0, TOP_K * r + s], 1), :], sems.at[s])
                for s in range(TOP_K)]

    def start(r, carry):
        for cp in copies(r):
            cp.start()
        return carry

    def wait(r, carry):
        for cp in copies(r):
            cp.wait()
        return carry

    lax.fori_loop(0, tt, start, 0)
    lax.fori_loop(0, tt, wait, 0)


def _moe_scatter(hp, pos, n_rows):
    m, dp = hp.shape
    tt = _pick(m, (192, 128, 64, 32, 16))
    pos3 = pos.reshape(m // tt, 1, TOP_K * tt)
    return pl.pallas_call(
        functools.partial(_moe_scatter_kernel, tt=tt),
        out_shape=jax.ShapeDtypeStruct((n_rows, dp), hp.dtype),
        grid=(m // tt,),
        in_specs=[pl.BlockSpec((1, 1, TOP_K * tt), lambda i: (i, 0, 0), memory_space=pltpu.SMEM),
                  pl.BlockSpec((tt, dp), lambda i: (i, 0)),
                  pl.BlockSpec(memory_space=pl.ANY)],
        out_specs=pl.BlockSpec(memory_space=pl.ANY),
        scratch_shapes=[pltpu.SemaphoreType.DMA((TOP_K,))],
        input_output_aliases={2: 0},
        compiler_params=_params(("arbitrary",)),
        name="moe_scatter",
    )(pos3, hp, jnp.zeros((n_rows, dp), hp.dtype))


def _unpack_bf16_pair(xp):
    lo = pltpu.bitcast(xp << 16, F32).astype(BF16)
    hi = pltpu.bitcast(xp & jnp.uint32(0xFFFF0000), F32).astype(BF16)
    return lo, hi


def _over_valid_rows(ns, rc, sub, body):
    big = 8 * sub
    for b in range(rc // big):
        @pl.when(ns >= 8 * (b + 1))
        def _(b=b):
            body(pl.ds(b * big, big))

    off = (ns // 8) * big
    rem = ns % 8
    for bit in (4, 2, 1):
        @pl.when((rem & bit) != 0)
        def _(bit=bit, off=off):
            body(pl.ds(pl.multiple_of(off, sub), bit * sub))

        off = off + (rem & bit) * sub


def _moe_gu_kernel(nsub_ref, cx_ref, ce_ref, x_ref, wgl_ref, wgh_ref, wul_ref, wuh_ref, o_ref,
                   accg, accu, wb, *, nk, sub):
    del cx_ref, ce_ref
    c = pl.program_id(0)
    k = pl.program_id(2)
    ns = nsub_ref[c]

    @pl.when(ns > 0)
    def _():
        @pl.when(k == 0)
        def _():
            accg[...] = jnp.zeros_like(accg)
            accu[...] = jnp.zeros_like(accu)

        for n, w_ref in enumerate((wgl_ref, wgh_ref, wul_ref, wuh_ref)):
            wb[n] = w_ref[...].astype(BF16)

        def body(rows):
            lo, hi = _unpack_bf16_pair(x_ref[rows, :])
            accg[rows, :] += (jnp.dot(lo, wb[0], preferred_element_type=F32)
                              + jnp.dot(hi, wb[1], preferred_element_type=F32))
            accu[rows, :] += (jnp.dot(lo, wb[2], preferred_element_type=F32)
                              + jnp.dot(hi, wb[3], preferred_element_type=F32))

        _over_valid_rows(ns, accg.shape[0], sub, body)

        @pl.when(k == nk - 1)
        def _():
            g = accg[...]
            o_ref[...] = (g * _sigmoid(g) * accu[...]).astype(o_ref.dtype)


def _moe_down_kernel(nsub_ref, cx_ref, ce_ref, x_ref, w_ref, o_ref, wb, *, sub):
    del cx_ref, ce_ref
    c = pl.program_id(0)
    k = pl.program_id(2)
    ns = nsub_ref[c]

    @pl.when(ns > 0)
    def _():
        @pl.when(k == 0)
        def _():
            o_ref[...] = jnp.zeros_like(o_ref)

        wb[...] = w_ref[...].astype(BF16)

        def body(rows):
            o_ref[rows, :] += jnp.dot(x_ref[rows, :], wb[...], preferred_element_type=F32)

        _over_valid_rows(ns, o_ref.shape[0], sub, body)


def _moe_experts(xs, wgu, wdn, nsub, cx, ce, *, d, d_ff, rc, nch, sub):
    half = d // 2
    tkp = _pick(half, (512, 256, 128))
    nk = half // tkp
    tn = _pick(d_ff, (512, 256, 128))
    nj = d_ff // tn

    def frozen(c, a, last, nsub_r):
        return jnp.where(nsub_r[c] > 0, a, last)

    def x_map(c, j, k, nsub_r, cx_r, ce_r):
        return (cx_r[c], frozen(c, k, nk - 1, nsub_r))

    def w_map(col_off, row_off):
        def f(c, j, k, nsub_r, cx_r, ce_r):
            return (ce_r[c] * (d // tkp) + row_off // tkp + frozen(c, k, nk - 1, nsub_r),
                    col_off // tn + frozen(c, j, nj - 1, nsub_r))
        return f

    def o_map(c, j, k, nsub_r, cx_r, ce_r):
        return (cx_r[c], frozen(c, j, nj - 1, nsub_r))

    act = pl.pallas_call(
        functools.partial(_moe_gu_kernel, nk=nk, sub=sub),
        out_shape=jax.ShapeDtypeStruct((nch * rc, d_ff), BF16),
        grid_spec=pltpu.PrefetchScalarGridSpec(
            num_scalar_prefetch=3,
            grid=(nch, nj, nk),
            in_specs=[pl.BlockSpec((rc, tkp), x_map),
                      pl.BlockSpec((tkp, tn), w_map(0, 0)),
                      pl.BlockSpec((tkp, tn), w_map(0, half)),
                      pl.BlockSpec((tkp, tn), w_map(d_ff, 0)),
                      pl.BlockSpec((tkp, tn), w_map(d_ff, half))],
            out_specs=pl.BlockSpec((rc, tn), o_map),
            scratch_shapes=[pltpu.VMEM((rc, tn), F32), pltpu.VMEM((rc, tn), F32),
                            pltpu.VMEM((4, tkp, tn), BF16)]),
        compiler_params=_params(("arbitrary", "arbitrary", "arbitrary")),
        name="moe_gu",
    )(nsub, cx, ce, xs, wgu, wgu, wgu, wgu)

    tk2 = _pick(d_ff, (1024, 512, 256, 128))
    nk2 = d_ff // tk2
    tn2 = _pick(d, (1024, 512, 256, 128))
    nj2 = d // tn2

    def x2_map(c, j, k, nsub_r, cx_r, ce_r):
        return (cx_r[c], frozen(c, k, nk2 - 1, nsub_r))

    def w2_map(c, j, k, nsub_r, cx_r, ce_r):
        return (ce_r[c] * nk2 + frozen(c, k, nk2 - 1, nsub_r), frozen(c, j, nj2 - 1, nsub_r))

    def o2_map(c, j, k, nsub_r, cx_r, ce_r):
        return (cx_r[c], frozen(c, j, nj2 - 1, nsub_r))

    return pl.pallas_call(
        functools.partial(_moe_down_kernel, sub=sub),
        out_shape=jax.ShapeDtypeStruct((nch * rc, d), F32),
        grid_spec=pltpu.PrefetchScalarGridSpec(
            num_scalar_prefetch=3,
            grid=(nch, nj2, nk2),
            in_specs=[pl.BlockSpec((rc, tk2), x2_map),
                      pl.BlockSpec((tk2, tn2), w2_map)],
            out_specs=pl.BlockSpec((rc, tn2), o2_map),
            scratch_shapes=[pltpu.VMEM((tk2, tn2), BF16)]),
        compiler_params=_params(("arbitrary", "arbitrary", "arbitrary")),
        name="moe_down",
    )(nsub, cx, ce, act, wdn)


def _moe_combine_kernel(pos_ref, x_ref, wts_ref, y_ref, o_ref, ybuf, sems, *, tt):
    i = pl.program_id(0)
    del i

    def copies(r):
        return [pltpu.make_async_copy(y_ref.at[pl.ds(pos_ref[0, 0, TOP_K * r + s], 1), :],
                                      ybuf.at[s, pl.ds(r, 1), :], sems.at[s])
                for s in range(TOP_K)]

    def start(r, carry):
        for cp in copies(r):
            cp.start()
        return carry

    def wait(r, carry):
        for cp in copies(r):
            cp.wait()
        return carry

    lax.fori_loop(0, tt, start, 0)
    lax.fori_loop(0, tt, wait, 0)
    w = wts_ref[...]
    o_ref[...] = x_ref[...] + (w[:, 0:1] * ybuf[0] + w[:, 1:2] * ybuf[1])


def _moe_combine(x, wts, y, pos):
    m, d = x.shape
    tt = _pick(m, (192, 128, 64, 32, 16))
    pos3 = pos.reshape(m // tt, 1, TOP_K * tt)
    return pl.pallas_call(
        functools.partial(_moe_combine_kernel, tt=tt),
        out_shape=jax.ShapeDtypeStruct((m, d), F32),
        grid=(m // tt,),
        in_specs=[pl.BlockSpec((1, 1, TOP_K * tt), lambda i: (i, 0, 0), memory_space=pltpu.SMEM),
                  pl.BlockSpec((tt, d), lambda i: (i, 0)),
                  pl.BlockSpec((tt, LANES), lambda i: (i, 0)),
                  pl.BlockSpec(memory_space=pl.ANY)],
        out_specs=pl.BlockSpec((tt, d), lambda i: (i, 0)),
        scratch_shapes=[pltpu.VMEM((TOP_K, tt, d), F32), pltpu.SemaphoreType.DMA((TOP_K,))],
        compiler_params=_params(("arbitrary",)),
        name="moe_combine",
    )(pos3, x, wts, y)


def kernel(x_prompt, x_sample, state_conv, cache_kv, cache_win, page_table, norm_mix, norm_ffn, conv_w_in, conv_w, conv_w_out, ffn_w_gu, ffn_w_down, moe_router, moe_w_gu, moe_w_down, kv_norm, w_kv, k_norm, cmp_w1, cmp_w2, cmp_pe, w_qg, q_norm, w_o):
    batch, seq, d = x_prompt.shape
    dec_batch, dec_seq, _ = x_sample.shape
    n_pool, page_size = cache_kv.shape[:2]
    n_pages = page_table.shape[1]
    past_len = n_pages * page_size
    l_win = cache_win.shape[1]
    d_ff = ffn_w_down.shape[1]
    n_experts = moe_router.shape[2]
    n_heads = d // HEAD_DIM
    hpg = n_heads // N_KV
    sec_w = N_KV * HEAD_DIM
    m_prompt = batch * seq
    ms = dec_batch * dec_seq
    m = m_prompt + ms
    assert seq & (seq - 1) == 0 and dec_seq & (dec_seq - 1) == 0 and dec_seq >= CONV_W - 1
    assert m_prompt % ms == 0 and ms % 16 == 0 and seq % 128 == 0 and seq >= WINDOW + 128
    assert past_len % L_SLC == 0 and dec_seq <= L_SLC and page_size % L_SLC == 0
    assert norm_mix.shape[0] == 2 and l_win == WINDOW and past_len >= l_win

    x0 = jnp.concatenate([x_prompt.reshape(m_prompt, d), x_sample.reshape(ms, d)], axis=0)
    tm = _row_tile(m)

    (h0,) = _rmsnorm(x0, norm_mix[0:1])
    tn = _pick(d, (256, 128))
    tk = _pick(d, (1024, 512, 256, 128))
    b_gate, u = _matmul(
        h0, conv_w_in[0], col_offsets=(0, d, 2 * d), n_cols=d, tm=tm, tn=tn, tk=tk,
        epilogue=_epi_conv_in, out_shapes=[jax.ShapeDtypeStruct((m, d), F32)] * 2,
        out_specs=[_ij_spec(tm, tn)] * 2, name="conv_in")
    st = state_conv[0]
    zrow = jnp.zeros((dec_batch, dec_seq - 1, d), F32)
    s1 = jnp.concatenate([st[:, 1:2], zrow], axis=1).reshape(ms, d)
    s2 = jnp.concatenate([st[:, 0:1], st[:, 1:2], zrow[:, 1:]], axis=1).reshape(ms, d)
    z = _conv_gate(u, b_gate, conv_w[0], s1, s2, m_prompt=m_prompt, seq=seq, dec_seq=dec_seq)
    tn = _pick(d, (512, 256, 128))
    (x1,) = _matmul(
        z, conv_w_out[0], col_offsets=(0,), n_cols=d, tm=tm, tn=tn, tk=tk, epilogue=_epi_residual,
        extras=(x0,), extra_specs=(_ij_spec(tm, tn),), out_shapes=[jax.ShapeDtypeStruct((m, d), F32)],
        out_specs=[_ij_spec(tm, tn)], name="conv_out")
    u_p = u[:m_prompt].reshape(batch, seq, d)
    conv_prompt = u_p[:, seq - (CONV_W - 1):][None]
    conv_sample = u[m_prompt:].reshape(dec_batch, dec_seq, d)[:, dec_seq - (CONV_W - 1):][None]

    (h1,) = _rmsnorm(x1, norm_ffn[0:1])
    tnf = _pick(d_ff, (512, 256, 128))
    (act,) = _matmul(
        h1, ffn_w_gu[0], col_offsets=(0, d_ff), n_cols=d_ff, tm=tm, tn=tnf, tk=tk, epilogue=_epi_swiglu,
        out_shapes=[jax.ShapeDtypeStruct((m, d_ff), BF16)], out_specs=[_ij_spec(tm, tnf)], name="ffn_gu")
    tkf = _pick(d_ff, (1024, 512, 256, 128))
    (x2,) = _matmul(
        act, ffn_w_down[0], col_offsets=(0,), n_cols=d, tm=tm, tn=tn, tk=tkf, epilogue=_epi_residual,
        extras=(x1,), extra_specs=(_ij_spec(tm, tn),), out_shapes=[jax.ShapeDtypeStruct((m, d), F32)],
        out_specs=[_ij_spec(tm, tn)], name="ffn_down")

    hkv, h2 = _rmsnorm(x2, jnp.stack([kv_norm, norm_mix[1]]))
    ones = jnp.ones((HEAD_DIM,), F32)
    kv_gain = jnp.stack([ones, ones, k_norm[1], ones, k_norm[2], ones]).reshape(2 * N_BRANCH, 1, HEAD_DIM)
    kv_f32, kv_bf = _matmul(
        hkv, w_kv, col_offsets=(0,), n_cols=2 * N_BRANCH * sec_w, tm=tm, tn=sec_w, tk=tk, epilogue=_epi_kv,
        extras=(kv_gain,), extra_specs=(pl.BlockSpec((None, 1, HEAD_DIM), lambda i, j, k: (j, 0, 0)),),
        out_shapes=[jax.ShapeDtypeStruct((m, 2 * N_BRANCH * sec_w), F32),
                    jax.ShapeDtypeStruct((m, 2 * N_BRANCH * sec_w), BF16)],
        out_specs=[_ij_spec(tm, sec_w)] * 2, name="kv_proj")
    kv_p = kv_f32[:m_prompt].reshape(batch, seq, 2 * N_BRANCH, N_KV, HEAD_DIM)
    kv_s = kv_f32[m_prompt:].reshape(dec_batch, dec_seq, 2 * N_BRANCH, N_KV, HEAD_DIM)
    kv_prompt = kv_p[:, :, :4]
    win_prompt = kv_p[:, seq - min(WINDOW, seq):, 4:]
    kv_sample = kv_s[:, :, :4]
    win_sample = jnp.concatenate([cache_win, kv_s[:, :, 4:]], axis=1)[:, dec_seq:]

    wq = w_qg[0]
    (q,) = _matmul(
        h2, wq, col_offsets=(0,), n_cols=d, tm=tm, tn=sec_w, tk=tk, epilogue=_epi_q,
        extras=(q_norm[0:1],), extra_specs=(pl.BlockSpec((1, HEAD_DIM), lambda i, j, k: (0, 0)),),
        out_shapes=[jax.ShapeDtypeStruct((m, d), BF16)], out_specs=[_ij_spec(tm, sec_w)], name="q_proj")
    wg = wq[:, d:].reshape(d, N_KV, hpg, N_BRANCH).transpose(0, 1, 3, 2).reshape(d, N_KV, N_BRANCH * hpg)
    wg = jnp.pad(wg, ((0, 0), (0, 0), (0, LANES - N_BRANCH * hpg))).reshape(d, N_KV * LANES)
    (gates,) = _matmul(
        h2, wg, col_offsets=(0,), n_cols=N_KV * LANES, tm=tm, tn=N_KV * LANES, tk=tk, epilogue=_epi_gate,
        out_shapes=[jax.ShapeDtypeStruct((m, N_KV * LANES), F32)], out_specs=[_ij_spec(tm, N_KV * LANES)],
        name="gate_proj")

    w1r = cmp_w1.reshape(2, R_CMP, STRIDE * HEAD_DIM, HEAD_DIM)
    per = cmp_pe.reshape(2, R_CMP, 1, STRIDE * HEAD_DIM)
    kn0 = k_norm[0:1]
    kc_p = _compress_prompt(kv_f32, w1r, per, cmp_w2, kn0, batch=batch, seq=seq)
    pt_flat = page_table.reshape(-1)
    kc_s = _compress_sample(cache_kv, pt_flat, w1r, per, cmp_w2, kn0, dec_batch=dec_batch, n_pages=n_pages,
                            page_size=page_size)

    o = _nsa_prompt(q, gates, kc_p, kv_bf, m_total=m, batch=batch, seq=seq, hpg=hpg)
    win2d = cache_win.reshape(dec_batch * l_win, 2 * sec_w)
    o = _nsa_sample(q, gates, kc_s, kv_f32, cache_kv, win2d, pt_flat, o, m_prompt=m_prompt,
                    dec_batch=dec_batch, dec_seq=dec_seq, hpg=hpg, past_len=past_len, n_pages=n_pages,
                    page_size=page_size, l_win=l_win)
    (x3,) = _matmul(
        o, w_o[0], col_offsets=(0,), n_cols=d, tm=tm, tn=tn, tk=tk, epilogue=_epi_residual,
        extras=(x2,), extra_specs=(_ij_spec(tm, tn),), out_shapes=[jax.ShapeDtypeStruct((m, d), F32)],
        out_specs=[_ij_spec(tm, tn)], name="attn_out")

    router_p = jnp.pad(moe_router[0], ((0, 0), (0, LANES - n_experts)))
    hp, ridx, wts = _router(x3, norm_ffn[1:2], router_p, n_experts)
    wgu = moe_w_gu[0].reshape(n_experts * d, 2 * d_ff)
    wdn = moe_w_down[0].reshape(n_experts * d_ff, d)
    sub = MOE_SUB_ROWS
    rc = -(-(TOP_K * m * 12 // (10 * n_experts)) // (2 * sub)) * (2 * sub)
    nch = TOP_K * m // rc + n_experts
    pos, nsub, cx, ce = _moe_plan(ridx[:, :TOP_K], n_experts, rc, nch, sub)
    xs = _moe_scatter(hp, pos, nch * rc)
    y = _moe_experts(xs, wgu, wdn, nsub, cx, ce, d=d, d_ff=d_ff, rc=rc, nch=nch, sub=sub)
    x4 = _moe_combine(x3, wts, y, pos)

    y_prompt = x4[:m_prompt].reshape(batch, seq, d)
    y_sample = x4[m_prompt:].reshape(dec_batch, dec_seq, d)
    return (y_prompt, y_sample, conv_prompt, kv_prompt, win_prompt, conv_sample, kv_sample, win_sample)
```

```python
import functools

import jax
import jax.numpy as jnp
from jax import lax
from jax.experimental import pallas as pl
from jax.experimental.pallas import tpu as pltpu

HEAD_DIM = 128
N_KV = 4
N_BRANCH = 3
L_CMP = 32
STRIDE = 16
R_CMP = L_CMP // STRIDE
L_SLC = 64
N_SEL = 16
WINDOW = 512
TOP_K = 2
CONV_W = 3
EPS = 1e-6
NEG = -1e30
FORCE = 1e6

LANES = 128
MOE_SUB_ROWS = 128
FLASH_SLAB_ROWS = 64
M_FLOOR = -1e29
LOG2_E = 1.4426950408889634
VMEM_LIMIT = 56 * 1024 * 1024

F32 = jnp.float32
BF16 = jnp.bfloat16


def _pick(n, candidates):
    for c in candidates:
        if c <= n and n % c == 0:
            return c
    return n


def _params(sem):
    return pltpu.CompilerParams(dimension_semantics=sem, vmem_limit_bytes=VMEM_LIMIT)


def _sigmoid(x):
    return 0.5 * jnp.tanh(0.5 * x) + 0.5


def _div_pow2(x, n):
    assert n & (n - 1) == 0
    return jnp.right_shift(x, n.bit_length() - 1)


def _rmsnorm_kernel(x_ref, g_ref, *o_refs):
    x = x_ref[...]
    y = x * lax.rsqrt(jnp.mean(x * x, axis=-1, keepdims=True) + EPS)
    for n, o_ref in enumerate(o_refs):
        o_ref[...] = (y * g_ref[n:n + 1, :]).astype(o_ref.dtype)


def _rmsnorm(x, gains):
    m, d = x.shape
    n = gains.shape[0]
    tm = _pick(m, (192, 128, 64, 32, 16))
    outs = pl.pallas_call(
        _rmsnorm_kernel,
        out_shape=[jax.ShapeDtypeStruct((m, d), BF16)] * n,
        grid=(m // tm,),
        in_specs=[pl.BlockSpec((tm, d), lambda i: (i, 0)),
                  pl.BlockSpec((n, d), lambda i: (0, 0))],
        out_specs=[pl.BlockSpec((tm, d), lambda i: (i, 0))] * n,
        compiler_params=_params(("parallel",)),
        name="rmsnorm",
    )(x, gains)
    return outs


def _mm_kernel(*refs, nk, n_w, n_extra, n_out, epilogue):
    x_ref = refs[0]
    w_refs = refs[1:1 + n_w]
    extra = refs[1 + n_w:1 + n_w + n_extra]
    out_refs = refs[1 + n_w + n_extra:1 + n_w + n_extra + n_out]
    acc_refs = refs[1 + n_w + n_extra + n_out:]
    i = pl.program_id(0)
    j = pl.program_id(1)
    k = pl.program_id(2)

    def step(first, last):
        x = x_ref[...]
        vals = []
        for w_ref, a in zip(w_refs, acc_refs):
            part = jnp.dot(x, w_ref[...].astype(BF16), preferred_element_type=F32)
            vals.append(part if first else a[...] + part)
        if last:
            epilogue((i, j), vals, extra, out_refs)
        else:
            for a, v in zip(acc_refs, vals):
                a[...] = v

    if nk == 1:
        step(True, True)
    else:
        pl.when(k == 0)(lambda: step(True, False))
        if nk > 2:
            pl.when((k > 0) & (k < nk - 1))(lambda: step(False, False))
        pl.when(k == nk - 1)(lambda: step(False, True))


def _matmul(x, w, *, col_offsets, n_cols, tm, tn, tk, epilogue, extras=(), extra_specs=(),
            out_shapes, out_specs, w_row_offset=0, name):
    m, kdim = x.shape
    nk = kdim // tk
    n_w = len(col_offsets)
    in_specs = [pl.BlockSpec((tm, tk), lambda i, j, k: (i, k))]
    for off in col_offsets:
        in_specs.append(pl.BlockSpec(
            (tk, tn), lambda i, j, k, off=off: (k + w_row_offset // tk, j + off // tn)))
    in_specs += list(extra_specs)
    kern = functools.partial(_mm_kernel, nk=nk, n_w=n_w, n_extra=len(extras),
                             n_out=len(out_shapes), epilogue=epilogue)
    return pl.pallas_call(
        kern,
        out_shape=out_shapes,
        grid=(m // tm, n_cols // tn, nk),
        in_specs=in_specs,
        out_specs=out_specs,
        scratch_shapes=[pltpu.VMEM((tm, tn), F32)] * n_w,
        compiler_params=_params(("parallel", "parallel", "arbitrary")),
        name=name,
    )(x, *([w] * n_w), *extras)


def _row_tile(m):
    return _pick(m, (2064, 2048, 1024, 688, 512, 256, 192, 128, 64, 32, 16))


def _ij_spec(tm, tn):
    return pl.BlockSpec((tm, tn), lambda i, j, k: (i, j))


def _epi_conv_in(ids, accs, extra, outs):
    b, c, v = accs
    outs[0][...] = b
    outs[1][...] = c * v


def _epi_residual(ids, accs, extra, outs):
    outs[0][...] = extra[0][...] + accs[0]


def _epi_swiglu(ids, accs, extra, outs):
    g, u = accs
    outs[0][...] = (g * _sigmoid(g) * u).astype(outs[0].dtype)


def _head_rmsnorm(a, gain):
    parts = []
    for h in range(a.shape[1] // HEAD_DIM):
        ah = a[:, h * HEAD_DIM:(h + 1) * HEAD_DIM]
        ms = jnp.mean(ah * ah, axis=-1, keepdims=True)
        parts.append(ah * lax.rsqrt(ms + EPS) * gain)
    return jnp.concatenate(parts, axis=1)


def _epi_kv(ids, accs, extra, outs):
    _, j = ids
    a = accs[0]
    normed = _head_rmsnorm(a, extra[0][...])
    y = jnp.where((j == 2) | (j == 4), normed, a)
    outs[0][...] = y
    outs[1][...] = y.astype(BF16)


def _epi_q(ids, accs, extra, outs):
    outs[0][...] = _head_rmsnorm(accs[0], extra[0][...]).astype(BF16)


def _epi_gate(ids, accs, extra, outs):
    outs[0][...] = _sigmoid(accs[0])


def _conv_kernel(u_ref, up_ref, b_ref, cw_ref, s1_ref, s2_ref, z_ref, *, tm, m_prompt, seq, dec_seq,
                 n_row_tiles):
    i = pl.program_id(0)
    u = u_ref[...]
    prev = up_ref[...]
    loc = lax.broadcasted_iota(jnp.int32, (tm, 1), 0)
    r = i * tm + loc
    u1 = pltpu.roll(u, 1, 0)
    u1 = jnp.where(loc == 0, prev[7:8, :], u1)
    u2 = pltpu.roll(u, 2, 0)
    u2 = jnp.where(loc == 0, prev[6:7, :], jnp.where(loc == 1, prev[7:8, :], u2))
    t = jnp.where(r < m_prompt, r & (seq - 1), (r - m_prompt) & (dec_seq - 1))
    u1 = jnp.where(t >= 1, u1, 0.0)
    u2 = jnp.where(t >= 2, u2, 0.0)
    w0 = cw_ref[0:1, :]
    w1 = cw_ref[1:2, :]
    w2 = cw_ref[2:3, :]
    conv = w2 * u + w1 * u1 + w0 * u2
    z_ref[...] = (b_ref[...] * conv).astype(z_ref.dtype)
    ms = s1_ref.shape[0]

    @pl.when(i == n_row_tiles - 1)
    def _():
        tail = conv[tm - ms:, :] + w1 * s1_ref[...] + w0 * s2_ref[...]
        z_ref[tm - ms:, :] = (b_ref[tm - ms:, :] * tail).astype(z_ref.dtype)


def _conv_gate(u, b, cw, s1, s2, *, m_prompt, seq, dec_seq):
    m, d = u.shape
    ms = s1.shape[0]
    tm = _row_tile(m)
    tc = _pick(d, (512, 256, 128))
    n_row_tiles = m // tm
    kern = functools.partial(_conv_kernel, tm=tm, m_prompt=m_prompt, seq=seq, dec_seq=dec_seq,
                             n_row_tiles=n_row_tiles)
    return pl.pallas_call(
        kern,
        out_shape=jax.ShapeDtypeStruct((m, d), BF16),
        grid=(n_row_tiles, d // tc),
        in_specs=[pl.BlockSpec((tm, tc), lambda i, j: (i, j)),
                  pl.BlockSpec((8, tc), lambda i, j: (jnp.maximum(i * (tm // 8) - 1, 0), j)),
                  pl.BlockSpec((tm, tc), lambda i, j: (i, j)),
                  pl.BlockSpec((CONV_W, tc), lambda i, j: (0, j)),
                  pl.BlockSpec((ms, tc), lambda i, j: (0, j)),
                  pl.BlockSpec((ms, tc), lambda i, j: (0, j))],
        out_specs=pl.BlockSpec((tm, tc), lambda i, j: (i, j)),
        compiler_params=_params(("parallel", "parallel")),
        name="conv_gate",
    )(u, u, b, cw, s1, s2)


def _cmp_stage1_compute(head_refs, w1_ref, pe_ref, top_ref, bot_ref, nch):
    for sec in range(2):
        wt = w1_ref[sec, 0].astype(BF16)
        wb = w1_ref[sec, 1].astype(BF16)
        pt = pe_ref[sec, 0]
        pb = pe_ref[sec, 1]
        rows = []
        for g in range(N_KV):
            ref = head_refs[sec * N_KV + g]
            cols = [ref[pl.ds(s, nch, stride=STRIDE), :] for s in range(STRIDE)]
            rows.append(jnp.concatenate(cols, axis=1))
        a = jnp.concatenate(rows, axis=0)
        top = jnp.dot((a + pt).astype(BF16), wt, preferred_element_type=F32)
        bot = jnp.dot((a + pb).astype(BF16), wb, preferred_element_type=F32)
        for g in range(N_KV):
            hh = sec * N_KV + g
            top_ref[:, hh * HEAD_DIM:(hh + 1) * HEAD_DIM] = top[g * nch:(g + 1) * nch, :]
            bot_ref[:, hh * HEAD_DIM:(hh + 1) * HEAD_DIM] = bot[g * nch:(g + 1) * nch, :]


def _cmp1_prompt_kernel(*refs, nch):
    n_heads = 2 * N_KV
    head_refs = refs[:n_heads]
    w1_ref, pe_ref, top_ref, bot_ref = refs[n_heads:]
    _cmp_stage1_compute(head_refs, w1_ref, pe_ref, top_ref, bot_ref, nch)


def _cmp1_sample_kernel(pt_ref, cache_ref, w1_ref, pe_ref, top_ref, bot_ref, buf_ref, sem, *,
                        pages_per_step, n_pages, page_size):
    b = pl.program_id(0)
    grp = pl.program_id(1)
    n_heads = 2 * N_KV

    def copy(p, hh):
        page = pt_ref[b * n_pages + grp * pages_per_step + p]
        return pltpu.make_async_copy(
            cache_ref.at[page, :, hh // N_KV, hh % N_KV, :],
            buf_ref.at[hh, pl.ds(p * page_size, page_size), :],
            sem.at[0])

    for p in range(pages_per_step):
        for hh in range(n_heads):
            copy(p, hh).start()
    for p in range(pages_per_step):
        for hh in range(n_heads):
            copy(p, hh).wait()
    _cmp_stage1_compute([buf_ref.at[hh] for hh in range(n_heads)], w1_ref, pe_ref, top_ref, bot_ref,
                        pages_per_step * page_size // STRIDE)


def _cmp2_kernel(top_ref, bot_ref, w2_ref, kn_ref, o_ref, *, nch):
    pre = top_ref[...] + pltpu.roll(bot_ref[...], nch - 1, 0)
    a = (pre * _sigmoid(pre)).astype(BF16)
    for sec in range(2):
        w2 = w2_ref[sec].astype(BF16)
        for g in range(N_KV):
            hh = sec * N_KV + g
            y = jnp.dot(a[:, hh * HEAD_DIM:(hh + 1) * HEAD_DIM], w2, preferred_element_type=F32)
            if sec == 0:
                y = _head_rmsnorm(y, kn_ref[...])
            o_ref[:, hh * HEAD_DIM:(hh + 1) * HEAD_DIM] = y.astype(o_ref.dtype)


def _cmp_stage2(top, bot, w2, kn0, nch):
    rows, width = top.shape
    return pl.pallas_call(
        functools.partial(_cmp2_kernel, nch=nch),
        out_shape=jax.ShapeDtypeStruct((rows, width), BF16),
        grid=(rows // nch,),
        in_specs=[pl.BlockSpec((nch, width), lambda b: (b, 0)),
                  pl.BlockSpec((nch, width), lambda b: (b, 0)),
                  pl.BlockSpec(w2.shape, lambda b: (0, 0, 0)),
                  pl.BlockSpec(kn0.shape, lambda b: (0, 0))],
        out_specs=pl.BlockSpec((nch, width), lambda b: (b, 0)),
        compiler_params=_params(("parallel",)),
        name="cmp_stage2",
    )(top, bot, w2, kn0)


def _compress_prompt(kv_f32, w1r, per, w2, kn0, *, batch, seq):
    width = 2 * N_KV * HEAD_DIM
    nch = seq // STRIDE
    top, bot = pl.pallas_call(
        functools.partial(_cmp1_prompt_kernel, nch=nch),
        out_shape=[jax.ShapeDtypeStruct((batch * nch, width), F32)] * 2,
        grid=(batch,),
        in_specs=[pl.BlockSpec((seq, HEAD_DIM), lambda b, hh=hh: (b, hh)) for hh in range(2 * N_KV)]
        + [pl.BlockSpec(w1r.shape, lambda b: (0, 0, 0, 0)),
           pl.BlockSpec(per.shape, lambda b: (0, 0, 0, 0))],
        out_specs=[pl.BlockSpec((nch, width), lambda b: (b, 0))] * 2,
        compiler_params=_params(("parallel",)),
        name="cmp_stage1_prompt",
    )(*([kv_f32] * (2 * N_KV)), w1r, per)
    return _cmp_stage2(top, bot, w2, kn0, nch)


def _compress_sample(cache2d, pt_flat, w1r, per, w2, kn0, *, dec_batch, n_pages, page_size):
    width = 2 * N_KV * HEAD_DIM
    pages_per_step = _pick(n_pages, (16, 8, 4, 2, 1))
    n_groups = n_pages // pages_per_step
    rows_step = pages_per_step * page_size
    nch_step = rows_step // STRIDE
    nch = n_pages * page_size // STRIDE
    kern = functools.partial(_cmp1_sample_kernel, pages_per_step=pages_per_step, n_pages=n_pages,
                             page_size=page_size)
    top, bot = pl.pallas_call(
        kern,
        out_shape=[jax.ShapeDtypeStruct((dec_batch * nch, width), F32)] * 2,
        grid_spec=pltpu.PrefetchScalarGridSpec(
            num_scalar_prefetch=1,
            grid=(dec_batch, n_groups),
            in_specs=[pl.BlockSpec(memory_space=pl.ANY),
                      pl.BlockSpec(w1r.shape, lambda b, g, pt: (0, 0, 0, 0)),
                      pl.BlockSpec(per.shape, lambda b, g, pt: (0, 0, 0, 0))],
            out_specs=[pl.BlockSpec((nch_step, width), lambda b, g, pt: (b * n_groups + g, 0))] * 2,
            scratch_shapes=[pltpu.VMEM((2 * N_KV, rows_step, HEAD_DIM), F32),
                            pltpu.SemaphoreType.DMA((1,))]),
        compiler_params=_params(("arbitrary", "arbitrary")),
        name="cmp_stage1_sample",
    )(pt_flat, cache2d, w1r, per)
    return _cmp_stage2(top, bot, w2, kn0, nch)


def _split3(x):
    hi = x.astype(BF16)
    r1 = x - hi.astype(F32)
    mid = r1.astype(BF16)
    lo = (r1 - mid.astype(F32)).astype(BF16)
    return hi, mid, lo


def _dot_f32ish(x, m_bf16):
    hi, mid, lo = _split3(x)
    out = jnp.dot(lo, m_bf16, preferred_element_type=F32)
    out = out + jnp.dot(mid, m_bf16, preferred_element_type=F32)
    return out + jnp.dot(hi, m_bf16, preferred_element_type=F32)


def _inter_t(n_c, n_s_lanes):
    c0 = lax.broadcasted_iota(jnp.int32, (n_c, n_s_lanes), 0) * STRIDE
    s0 = lax.broadcasted_iota(jnp.int32, (n_c, n_s_lanes), 1) * L_SLC
    return ((c0 < s0 + L_SLC) & (c0 + L_CMP > s0)).astype(BF16)


def _rank_desc(score, ns):
    lane = lax.broadcasted_iota(jnp.int32, score.shape, 1)
    rank = jnp.zeros(score.shape, jnp.int32)
    for sp in range(ns):
        col = score[:, sp:sp + 1]
        ahead = (col > score) | ((col == score) & (lane > sp))
        rank = rank + ahead.astype(jnp.int32)
    return rank


def _selection_scores(imp, qblk, ns):
    j = lax.broadcasted_iota(jnp.int32, imp.shape, 1)
    forced = (j == 0) | (j == qblk) | (j == qblk - 1)
    score = jnp.where(forced, FORCE, imp)
    score = jnp.where(j <= qblk, score, -FORCE)
    return jnp.where(j < ns, score, -2.0 * FORCE)


def _flash_init(m_sc, l_sc, acc_sc):
    m_sc[...] = jnp.full(m_sc.shape, M_FLOOR, F32)
    l_sc[...] = jnp.zeros(l_sc.shape, F32)
    acc_sc[...] = jnp.zeros(acc_sc.shape, F32)


def _flash_tile(q, kt, vt, *, s_sc, p_sc, bias_sc, m_sc, l_sc, alpha_sc, acc_sc, tq, width):
    c = (HEAD_DIM ** -0.5) * LOG2_E
    nt = (((1,), (1,)), ((), ()))
    reps = width // LANES
    s_sc[:, :width] = lax.dot_general(q, kt, nt, preferred_element_type=F32)
    slabs = [(slice(r0, r0 + FLASH_SLAB_ROWS), slice(r0 % tq, r0 % tq + FLASH_SLAB_ROWS))
             for r0 in range(0, q.shape[0], FLASH_SLAB_ROWS)]
    for rs, qs in slabs:
        s = s_sc[rs, :width] * c + bias_sc[qs, :width]
        s_sc[rs, :width] = s
        m_old = m_sc[rs, :]
        m_new = jnp.maximum(m_old, jnp.max(s, axis=-1, keepdims=True))
        alpha_sc[rs, :] = jnp.exp2(m_old - m_new)
        m_sc[rs, :] = m_new
    for rs, qs in slabs:
        m_b = jnp.concatenate([m_sc[rs, :]] * reps, axis=1)
        p = jnp.exp2(s_sc[rs, :width] - m_b)
        l_sc[rs, :] = alpha_sc[rs, :] * l_sc[rs, :] + jnp.sum(p, axis=-1, keepdims=True)
        p_sc[rs, :width] = p.astype(p_sc.dtype)
    pv = jnp.dot(p_sc[:, :width], vt, preferred_element_type=F32)
    acc_sc[...] = alpha_sc[...] * acc_sc[...] + pv


def _nsa_prompt_kernel(q_ref, g_ref, kc_ref, vc_ref, ks_ref, vs_ref, kw_ref, vw_ref, o_init_ref, o_ref,
                       s_sc, p_sc, bias_sc, m_sc, l_sc, alpha_sc, acc_sc, *, tq, tk, hpg, seq, ncp):
    del o_init_ref
    qt = pl.program_id(2)
    t0 = qt * tq
    scale = HEAD_DIM ** -0.5
    rows = hpg * tq
    qf = q_ref[...]
    q = jnp.concatenate([qf[:, h * HEAD_DIM:(h + 1) * HEAD_DIM] for h in range(hpg)], axis=0)
    qpos = t0 + lax.broadcasted_iota(jnp.int32, (tq, 1), 0)
    nt = (((1,), (1,)), ((), ()))

    s = lax.dot_general(q, kc_ref[...], nt, preferred_element_type=F32) * scale
    c_end = lax.broadcasted_iota(jnp.int32, (tq, ncp), 1) * STRIDE + (L_CMP - 1)
    mc = (c_end <= qpos)[None]
    s3 = jnp.where(mc, s.reshape(hpg, tq, ncp), NEG)
    e = jnp.where(mc, jnp.exp(s3 - jnp.max(s3, axis=-1, keepdims=True)), 0.0)
    den = jnp.sum(e, axis=-1, keepdims=True)
    p_c = jnp.where(den > 0.0, e / jnp.where(den > 0.0, den, 1.0), 0.0)
    o_c = jnp.dot(p_c.reshape(rows, ncp).astype(BF16), vc_ref[...], preferred_element_type=F32)
    psum = jnp.sum(p_c, axis=0)

    ns = -(-seq // L_SLC)
    imp = _dot_f32ish(psum, _inter_t(ncp, LANES))
    qblk = _div_pow2(qpos, L_SLC)
    score = _selection_scores(imp, qblk, ns)
    rank = _rank_desc(score, ns)
    sel = ((rank < N_SEL) & (score > -0.5 * FORCE)).astype(BF16)

    flash = functools.partial(_flash_tile, s_sc=s_sc, p_sc=p_sc, bias_sc=bias_sc, m_sc=m_sc, l_sc=l_sc,
                              alpha_sc=alpha_sc, acc_sc=acc_sc, tq=tq)
    _flash_init(m_sc, l_sc, acc_sc)

    def body(j, carry):
        k0 = pl.multiple_of(j * tk, tk)
        kpos = k0 + lax.broadcasted_iota(jnp.int32, (1, tk), 1)
        expand = (lax.broadcasted_iota(jnp.int32, (LANES, tk), 0) == _div_pow2(kpos, L_SLC)).astype(BF16)
        km = jnp.dot(sel, expand, preferred_element_type=F32) > 0.5
        bias_sc[:, :tk] = jnp.where(km & (kpos <= qpos), 0.0, NEG)
        flash(q, ks_ref[pl.ds(k0, tk), :], vs_ref[pl.ds(k0, tk), :], width=tk)
        return carry

    lax.fori_loop(0, (t0 + tq - 1) // tk + 1, body, 0)
    o_s = acc_sc[...] / l_sc[...]

    wl = WINDOW + tq
    w0 = pl.multiple_of(jnp.clip(t0 - WINDOW, 0, seq - wl), tq)
    wpos = w0 + lax.broadcasted_iota(jnp.int32, (1, wl), 1)
    dlt = qpos - wpos
    bias_sc[...] = jnp.where((dlt >= 0) & (dlt <= WINDOW), 0.0, NEG)
    _flash_init(m_sc, l_sc, acc_sc)
    flash(q, kw_ref[pl.ds(w0, wl), :], vw_ref[pl.ds(w0, wl), :], width=wl)
    o_w = acc_sc[...] / l_sc[...]

    gt = g_ref[...]
    for h in range(hpg):
        sl = slice(h * tq, (h + 1) * tq)
        g0 = gt[:, 0 * hpg + h:0 * hpg + h + 1]
        g1 = gt[:, 1 * hpg + h:1 * hpg + h + 1]
        g2 = gt[:, 2 * hpg + h:2 * hpg + h + 1]
        o_ref[:, h * HEAD_DIM:(h + 1) * HEAD_DIM] = (
            g0 * o_c[sl, :] + g1 * o_s[sl, :] + g2 * o_w[sl, :]).astype(o_ref.dtype)


def _nsa_prompt(q, gates, kc, kv_bf, *, m_total, batch, seq, hpg):
    d = q.shape[1]
    tq = 128
    tk = _pick(seq, (512, 256, 128))
    ncp = seq // STRIDE
    nqt = seq // tq
    gw = hpg * HEAD_DIM
    rows = hpg * tq
    wmax = max(tk, WINDOW + tq)
    kern = functools.partial(_nsa_prompt_kernel, tq=tq, tk=tk, hpg=hpg, seq=seq, ncp=ncp)

    def kvspec(sec):
        return pl.BlockSpec((seq, HEAD_DIM), lambda b, g, t, sec=sec: (b, sec * N_KV + g))

    return pl.pallas_call(
        kern,
        out_shape=jax.ShapeDtypeStruct((m_total, d), BF16),
        grid=(batch, N_KV, nqt),
        in_specs=[pl.BlockSpec((tq, gw), lambda b, g, t: (b * nqt + t, g)),
                  pl.BlockSpec((tq, LANES), lambda b, g, t: (b * nqt + t, g)),
                  pl.BlockSpec((ncp, HEAD_DIM), lambda b, g, t: (b, g)),
                  pl.BlockSpec((ncp, HEAD_DIM), lambda b, g, t: (b, N_KV + g)),
                  kvspec(2), kvspec(3), kvspec(4), kvspec(5),
                  pl.BlockSpec(memory_space=pl.ANY)],
        out_specs=pl.BlockSpec((tq, gw), lambda b, g, t: (b * nqt + t, g)),
        scratch_shapes=[pltpu.VMEM((rows, wmax), F32),
                        pltpu.VMEM((rows, wmax), BF16),
                        pltpu.VMEM((tq, wmax), F32),
                        pltpu.VMEM((rows, LANES), F32), pltpu.VMEM((rows, LANES), F32),
                        pltpu.VMEM((rows, LANES), F32), pltpu.VMEM((rows, HEAD_DIM), F32)],
        input_output_aliases={8: 0},
        compiler_params=_params(("parallel", "parallel", "arbitrary")),
        name="nsa_prompt",
    )(q, gates, kc, kc, kv_bf, kv_bf, kv_bf, kv_bf, jnp.zeros((m_total, d), BF16))


def _nsa_sample_select_kernel(q_ref, kc_ref, vc_ref, oc_ref, idx_ref, *, dec_batch, dec_seq, hpg,
                              past_len, ncp, ns, ns_lanes):
    scale = HEAD_DIM ** -0.5
    nt = (((1,), (1,)), ((), ()))
    qf = q_ref[...].astype(F32)
    rows = hpg * dec_seq
    qrow = lax.broadcasted_iota(jnp.int32, (dec_seq, 1), 0)
    qpos = past_len + qrow
    c_end = lax.broadcasted_iota(jnp.int32, (dec_seq, ncp), 1) * STRIDE + (L_CMP - 1)
    mc = (c_end <= qpos)[None]
    psums = []
    for b in range(dec_batch):
        qb = qf[b * dec_seq:(b + 1) * dec_seq, :]
        q = jnp.concatenate([qb[:, h * HEAD_DIM:(h + 1) * HEAD_DIM] for h in range(hpg)],
                            axis=0).astype(BF16)
        kc = kc_ref[b * ncp:(b + 1) * ncp, :]
        vc = vc_ref[b * ncp:(b + 1) * ncp, :]
        s = lax.dot_general(q, kc, nt, preferred_element_type=F32) * scale
        s3 = jnp.where(mc, s.reshape(hpg, dec_seq, ncp), NEG)
        e = jnp.where(mc, jnp.exp(s3 - jnp.max(s3, axis=-1, keepdims=True)), 0.0)
        den = jnp.sum(e, axis=-1, keepdims=True)
        p_c = jnp.where(den > 0.0, e / jnp.where(den > 0.0, den, 1.0), 0.0)
        o_c = jnp.dot(p_c.reshape(rows, ncp).astype(BF16), vc, preferred_element_type=F32)
        for h in range(hpg):
            oc_ref[b * dec_seq:(b + 1) * dec_seq, h * HEAD_DIM:(h + 1) * HEAD_DIM] = (
                o_c[h * dec_seq:(h + 1) * dec_seq, :])
        psums.append(jnp.sum(p_c, axis=0))
    psum = jnp.concatenate(psums, axis=0)
    imp = _dot_f32ish(psum, _inter_t(ncp, ns_lanes))
    n_rows = dec_batch * dec_seq
    qpos_all = past_len + (lax.broadcasted_iota(jnp.int32, (n_rows, 1), 0) & (dec_seq - 1))
    score = _selection_scores(imp, _div_pow2(qpos_all, L_SLC), ns)
    rank = _rank_desc(score, ns)
    ok = score > -0.5 * FORCE
    lane = lax.broadcasted_iota(jnp.int32, score.shape, 1)
    out_lane = lax.broadcasted_iota(jnp.int32, (n_rows, LANES), 1)
    idx = jnp.full((n_rows, LANES), -1, jnp.int32)
    for k in range(N_SEL):
        hit = (rank == k) & ok
        blk = jnp.sum(jnp.where(hit, (lane + 1).astype(F32), 0.0), axis=-1, keepdims=True)
        blk = blk.astype(jnp.int32) - 1
        idx = jnp.where(out_lane == k, blk, idx)
    idx_ref[...] = idx


def _nsa_sample_attend_kernel(idx_ref, pt_ref, q_ref, g_ref, oc_ref, knew_ref, vnew_ref, kwo_ref, vwo_ref,
                              kwn_ref, vwn_ref, cache_ref, o_in_ref, o_ref, kbuf, vbuf, of32, sems, *,
                              dec_batch, dec_seq, hpg, past_len, n_pages, page_size, l_win):
    del o_in_ref
    g = pl.program_id(0)
    scale = HEAD_DIM ** -0.5
    nt = (((1,), (1,)), ((), ()))
    rows = hpg * dec_seq
    n_past_blocks = past_len // L_SLC
    blocks_per_page = page_size // L_SLC
    nkeys = N_SEL * L_SLC
    qf = q_ref[...].astype(F32)
    gt = g_ref[...]
    qrow = lax.broadcasted_iota(jnp.int32, (rows, 1), 0) & (dec_seq - 1)
    key_slot = _div_pow2(lax.broadcasted_iota(jnp.int32, (1, nkeys), 1), L_SLC)
    newj = lax.broadcasted_iota(jnp.int32, (1, dec_seq), 1)

    for b in range(dec_batch):
        def copies(qi, k):
            blk = idx_ref[((g * dec_batch + b) * dec_seq + qi) * N_SEL + k]
            blk = jnp.clip(blk, 0, n_past_blocks - 1)
            page = pt_ref[b * n_pages + blk // blocks_per_page]
            row0 = pl.multiple_of((blk % blocks_per_page) * L_SLC, L_SLC)
            ck = pltpu.make_async_copy(cache_ref.at[page, pl.ds(row0, L_SLC), 2, g, :],
                                       kbuf.at[qi, pl.ds(k * L_SLC, L_SLC), :], sems.at[0])
            cv = pltpu.make_async_copy(cache_ref.at[page, pl.ds(row0, L_SLC), 3, g, :],
                                       vbuf.at[qi, pl.ds(k * L_SLC, L_SLC), :], sems.at[1])
            return ck, cv

        for qi in range(dec_seq):
            for k in range(N_SEL):
                ck, cv = copies(qi, k)
                ck.start()
                cv.start()
        for qi in range(dec_seq):
            for k in range(N_SEL):
                ck, cv = copies(qi, k)
                ck.wait()
                cv.wait()

        qb = qf[b * dec_seq:(b + 1) * dec_seq, :]
        q = jnp.concatenate([qb[:, h * HEAD_DIM:(h + 1) * HEAD_DIM] for h in range(hpg)],
                            axis=0).astype(BF16)
        knew = knew_ref[b * dec_seq:(b + 1) * dec_seq, :].astype(BF16)
        vnew = vnew_ref[b * dec_seq:(b + 1) * dec_seq, :].astype(BF16)
        s_new = lax.dot_general(q, knew, nt, preferred_element_type=F32) * scale

        o_s = jnp.zeros((rows, HEAD_DIM), F32)
        for qi in range(dec_seq):
            valid = jnp.zeros((1, nkeys), jnp.int32)
            has_new = jnp.int32(0)
            for k in range(N_SEL):
                blk = idx_ref[((g * dec_batch + b) * dec_seq + qi) * N_SEL + k]
                is_past = ((blk >= 0) & (blk < n_past_blocks)).astype(jnp.int32)
                valid = jnp.where(key_slot == k, is_past, valid)
                has_new = has_new | (blk == n_past_blocks).astype(jnp.int32)
            mk = valid > 0
            mn = ((newj <= qi).astype(jnp.int32) * has_new) > 0
            kq = kbuf[qi].astype(BF16)
            vq = vbuf[qi].astype(BF16)
            s_old = lax.dot_general(q, kq, nt, preferred_element_type=F32) * scale
            s_old = jnp.where(mk, s_old, NEG)
            s_n = jnp.where(mn, s_new, NEG)
            mx = jnp.maximum(jnp.max(s_old, axis=-1, keepdims=True), jnp.max(s_n, axis=-1, keepdims=True))
            p_old = jnp.where(mk, jnp.exp(s_old - mx), 0.0)
            p_n = jnp.where(mn, jnp.exp(s_n - mx), 0.0)
            den = jnp.sum(p_old, axis=-1, keepdims=True) + jnp.sum(p_n, axis=-1, keepdims=True)
            o = (jnp.dot(p_old.astype(BF16), vq, preferred_element_type=F32)
                 + jnp.dot(p_n.astype(BF16), vnew, preferred_element_type=F32)) / den
            o_s = jnp.where(qrow == qi, o, o_s)

        kwo = kwo_ref[b * l_win:(b + 1) * l_win, :].astype(BF16)
        vwo = vwo_ref[b * l_win:(b + 1) * l_win, :].astype(BF16)
        kwn = kwn_ref[b * dec_seq:(b + 1) * dec_seq, :].astype(BF16)
        vwn = vwn_ref[b * dec_seq:(b + 1) * dec_seq, :].astype(BF16)
        qpos = past_len + qrow
        wpos_o = past_len - l_win + lax.broadcasted_iota(jnp.int32, (1, l_win), 1)
        wpos_n = past_len + newj
        d_o = qpos - wpos_o
        d_n = qpos - wpos_n
        m_o = (d_o >= 0) & (d_o <= WINDOW) & (wpos_o >= 0)
        m_n = (d_n >= 0) & (d_n <= WINDOW)
        sw_o = jnp.where(m_o, lax.dot_general(q, kwo, nt, preferred_element_type=F32) * scale, NEG)
        sw_n = jnp.where(m_n, lax.dot_general(q, kwn, nt, preferred_element_type=F32) * scale, NEG)
        mx = jnp.maximum(jnp.max(sw_o, axis=-1, keepdims=True), jnp.max(sw_n, axis=-1, keepdims=True))
        pw_o = jnp.where(m_o, jnp.exp(sw_o - mx), 0.0)
        pw_n = jnp.where(m_n, jnp.exp(sw_n - mx), 0.0)
        den = jnp.sum(pw_o, axis=-1, keepdims=True) + jnp.sum(pw_n, axis=-1, keepdims=True)
        o_w = (jnp.dot(pw_o.astype(BF16), vwo, preferred_element_type=F32)
               + jnp.dot(pw_n.astype(BF16), vwn, preferred_element_type=F32)) / den

        gb = gt[b * dec_seq:(b + 1) * dec_seq, :]
        for h in range(hpg):
            sl = slice(h * dec_seq, (h + 1) * dec_seq)
            g0 = gb[:, 0 * hpg + h:0 * hpg + h + 1]
            g1 = gb[:, 1 * hpg + h:1 * hpg + h + 1]
            g2 = gb[:, 2 * hpg + h:2 * hpg + h + 1]
            o_c = oc_ref[b * dec_seq:(b + 1) * dec_seq, h * HEAD_DIM:(h + 1) * HEAD_DIM]
            of32[b * dec_seq:(b + 1) * dec_seq, h * HEAD_DIM:(h + 1) * HEAD_DIM] = (
                g0 * o_c + g1 * o_s[sl, :] + g2 * o_w[sl, :])
    o_ref[...] = of32[...].astype(o_ref.dtype)


def _nsa_sample(q, gates, kc_s, kv_f32, cache2d, win2d, pt_flat, o_prompt, *, m_prompt, dec_batch,
                dec_seq, hpg, past_len, n_pages, page_size, l_win):
    ms = dec_batch * dec_seq
    gw = hpg * HEAD_DIM
    ncp = past_len // STRIDE
    t_all = past_len + dec_seq
    ns = -(-t_all // L_SLC)
    ns_lanes = -(-ns // LANES) * LANES
    rb = m_prompt // ms
    sel = functools.partial(_nsa_sample_select_kernel, dec_batch=dec_batch, dec_seq=dec_seq, hpg=hpg,
                            past_len=past_len, ncp=ncp, ns=ns, ns_lanes=ns_lanes)
    oc, idx = pl.pallas_call(
        sel,
        out_shape=[jax.ShapeDtypeStruct((ms, N_KV * gw), F32),
                   jax.ShapeDtypeStruct((N_KV * ms, LANES), jnp.int32)],
        grid=(N_KV,),
        in_specs=[pl.BlockSpec((ms, gw), lambda g: (rb, g)),
                  pl.BlockSpec((dec_batch * ncp, HEAD_DIM), lambda g: (0, g)),
                  pl.BlockSpec((dec_batch * ncp, HEAD_DIM), lambda g: (0, N_KV + g))],
        out_specs=[pl.BlockSpec((ms, gw), lambda g: (0, g)),
                   pl.BlockSpec((ms, LANES), lambda g: (g, 0))],
        compiler_params=_params(("parallel",)),
        name="nsa_sample_select",
    )(q, kc_s, kc_s)
    idx_flat = idx[:, :N_SEL].reshape(-1)

    att = functools.partial(_nsa_sample_attend_kernel, dec_batch=dec_batch, dec_seq=dec_seq, hpg=hpg,
                            past_len=past_len, n_pages=n_pages, page_size=page_size, l_win=l_win)

    def sm(f):
        return lambda g, idx_r, pt_r: f(g)

    return pl.pallas_call(
        att,
        out_shape=jax.ShapeDtypeStruct(o_prompt.shape, o_prompt.dtype),
        grid_spec=pltpu.PrefetchScalarGridSpec(
            num_scalar_prefetch=2,
            grid=(N_KV,),
            in_specs=[pl.BlockSpec((ms, gw), sm(lambda g: (rb, g))),
                      pl.BlockSpec((ms, LANES), sm(lambda g: (rb, g))),
                      pl.BlockSpec((ms, gw), sm(lambda g: (0, g))),
                      pl.BlockSpec((ms, HEAD_DIM), sm(lambda g: (rb, 2 * N_KV + g))),
                      pl.BlockSpec((ms, HEAD_DIM), sm(lambda g: (rb, 3 * N_KV + g))),
                      pl.BlockSpec((dec_batch * l_win, HEAD_DIM), sm(lambda g: (0, g))),
                      pl.BlockSpec((dec_batch * l_win, HEAD_DIM), sm(lambda g: (0, N_KV + g))),
                      pl.BlockSpec((ms, HEAD_DIM), sm(lambda g: (rb, 4 * N_KV + g))),
                      pl.BlockSpec((ms, HEAD_DIM), sm(lambda g: (rb, 5 * N_KV + g))),
                      pl.BlockSpec(memory_space=pl.ANY),
                      pl.BlockSpec(memory_space=pl.ANY)],
            out_specs=pl.BlockSpec((ms, gw), sm(lambda g: (rb, g))),
            scratch_shapes=[pltpu.VMEM((dec_seq, N_SEL * L_SLC, HEAD_DIM), F32),
                            pltpu.VMEM((dec_seq, N_SEL * L_SLC, HEAD_DIM), F32),
                            pltpu.VMEM((ms, gw), F32),
                            pltpu.SemaphoreType.DMA((2,))]),
        input_output_aliases={12: 0},
        compiler_params=_params(("arbitrary",)),
        name="nsa_sample_attend",
    )(idx_flat, pt_flat, q, gates, oc, kv_f32, kv_f32, win2d, win2d, kv_f32, kv_f32, cache2d, o_prompt)


def _router_kernel(x_ref, g_ref, r_ref, hp_ref, idx_ref, wts_ref, *, n_experts):
    x = x_ref[...]
    h = x * lax.rsqrt(jnp.mean(x * x, axis=-1, keepdims=True) + EPS) * g_ref[...]
    half = h.shape[1] // 2
    bits = pltpu.bitcast(h.astype(BF16).astype(F32), jnp.uint32)
    hp_ref[...] = (bits[:, :half] >> 16) | (bits[:, half:] & jnp.uint32(0xFFFF0000))
    r_hi, r_mid, r_lo = _split3(r_ref[...])
    h_hi, h_mid, h_lo = _split3(h)

    def d(a, b):
        return jnp.dot(a, b, preferred_element_type=F32)

    logits = (d(h_lo, r_hi) + d(h_hi, r_lo) + d(h_mid, r_mid)) + (d(h_mid, r_hi) + d(h_hi, r_mid)) + d(h_hi, r_hi)
    lane = lax.broadcasted_iota(jnp.int32, logits.shape, 1)
    logits = jnp.where(lane < n_experts, logits, -jnp.inf)
    v1 = jnp.max(logits, axis=-1, keepdims=True)
    i1 = jnp.min(jnp.where(logits == v1, lane, LANES), axis=-1, keepdims=True)
    rest = jnp.where(lane == i1, -jnp.inf, logits)
    v2 = jnp.max(rest, axis=-1, keepdims=True)
    i2 = jnp.min(jnp.where(rest == v2, lane, LANES), axis=-1, keepdims=True)
    e2 = jnp.exp(v2 - v1)
    w1 = 1.0 / (1.0 + e2)
    w2 = e2 / (1.0 + e2)
    idx_ref[...] = jnp.where(lane == 0, i1, jnp.where(lane == 1, i2, 0))
    wts_ref[...] = jnp.where(lane == 0, w1, jnp.where(lane == 1, w2, 0.0))


def _router(x, gain, router_p, n_experts):
    m, d = x.shape
    tm = _pick(m, (192, 128, 64, 32, 16))
    return pl.pallas_call(
        functools.partial(_router_kernel, n_experts=n_experts),
        out_shape=[jax.ShapeDtypeStruct((m, d // 2), jnp.uint32),
                   jax.ShapeDtypeStruct((m, LANES), jnp.int32),
                   jax.ShapeDtypeStruct((m, LANES), F32)],
        grid=(m // tm,),
        in_specs=[pl.BlockSpec((tm, d), lambda i: (i, 0)),
                  pl.BlockSpec((1, d), lambda i: (0, 0)),
                  pl.BlockSpec((d, LANES), lambda i: (0, 0))],
        out_specs=[pl.BlockSpec((tm, d // 2), lambda i: (i, 0)),
                   pl.BlockSpec((tm, LANES), lambda i: (i, 0)),
                   pl.BlockSpec((tm, LANES), lambda i: (i, 0))],
        compiler_params=_params(("parallel",)),
        name="moe_router",
    )(x, gain, router_p)


def _moe_plan(idx2, n_experts, rc, nch, sub):
    e = idx2.reshape(-1)
    onehot = (e[:, None] == jnp.arange(n_experts, dtype=jnp.int32)[None, :]).astype(jnp.int32)
    csum = jnp.cumsum(onehot, axis=0)
    rank = jnp.take_along_axis(csum, e[:, None], axis=1)[:, 0] - 1
    counts = csum[-1]
    nchunks = (counts + rc - 1) // rc
    cend = jnp.cumsum(nchunks)
    cstart = cend - nchunks
    pos = (cstart[e] + rank // rc) * rc + rank % rc
    n_active = cend[-1]
    c = jnp.arange(nch, dtype=jnp.int32)
    cx = jnp.minimum(c, n_active - 1)
    ce = jnp.sum((cend[None, :] <= cx[:, None]).astype(jnp.int32), axis=1)
    rows = jnp.clip(counts[ce] - (cx - cstart[ce]) * rc, 0, rc)
    nsub = jnp.where(c < n_active, (rows + sub - 1) // sub, 0)
    return pos.astype(jnp.int32), nsub.astype(jnp.int32), cx.astype(jnp.int32), ce


def _moe_scatter_kernel(pos_ref, hp_ref, xs_in_ref, xs_ref, sems, *, tt):
    del xs_in_ref

    def copies(r):
        return [pltpu.make_async_copy(hp_ref.at[pl.ds(r, 1), :],
                                      xs_ref.at[pl.ds(pos_ref[0, 0, TOP_K * r + s], 1), :], sems.at[s])
                for s in range(TOP_K)]

    def start(r, carry):
        for cp in copies(r):
            cp.start()
        return carry

    def wait(r, carry):
        for cp in copies(r):
            cp.wait()
        return carry

    lax.fori_loop(0, tt, start, 0)
    lax.fori_loop(0, tt, wait, 0)


def _moe_scatter(hp, pos, n_rows):
    m, dp = hp.shape
    tt = _pick(m, (192, 128, 64, 32, 16))
    pos3 = pos.reshape(m // tt, 1, TOP_K * tt)
    return pl.pallas_call(
        functools.partial(_moe_scatter_kernel, tt=tt),
        out_shape=jax.ShapeDtypeStruct((n_rows, dp), hp.dtype),
        grid=(m // tt,),
        in_specs=[pl.BlockSpec((1, 1, TOP_K * tt), lambda i: (i, 0, 0), memory_space=pltpu.SMEM),
                  pl.BlockSpec((tt, dp), lambda i: (i, 0)),
                  pl.BlockSpec(memory_space=pl.ANY)],
        out_specs=pl.BlockSpec(memory_space=pl.ANY),
        scratch_shapes=[pltpu.SemaphoreType.DMA((TOP_K,))],
        input_output_aliases={2: 0},
        compiler_params=_params(("arbitrary",)),
        name="moe_scatter",
    )(pos3, hp, jnp.zeros((n_rows, dp), hp.dtype))


def _unpack_bf16_pair(xp):
    lo = pltpu.bitcast(xp << 16, F32).astype(BF16)
    hi = pltpu.bitcast(xp & jnp.uint32(0xFFFF0000), F32).astype(BF16)
    return lo, hi


def _over_valid_rows(ns, rc, sub, body):
    big = 8 * sub
    for b in range(rc // big):
        @pl.when(ns >= 8 * (b + 1))
        def _(b=b):
            body(pl.ds(b * big, big))

    off = (ns // 8) * big
    rem = ns % 8
    for bit in (4, 2, 1):
        @pl.when((rem & bit) != 0)
        def _(bit=bit, off=off):
            body(pl.ds(pl.multiple_of(off, sub), bit * sub))

        off = off + (rem & bit) * sub


def _moe_gu_kernel(nsub_ref, cx_ref, ce_ref, x_ref, wgl_ref, wgh_ref, wul_ref, wuh_ref, o_ref,
                   accg, accu, wb, *, nk, sub):
    del cx_ref, ce_ref
    c = pl.program_id(0)
    k = pl.program_id(2)
    ns = nsub_ref[c]

    @pl.when(ns > 0)
    def _():
        @pl.when(k == 0)
        def _():
            accg[...] = jnp.zeros_like(accg)
            accu[...] = jnp.zeros_like(accu)

        for n, w_ref in enumerate((wgl_ref, wgh_ref, wul_ref, wuh_ref)):
            wb[n] = w_ref[...].astype(BF16)

        def body(rows):
            lo, hi = _unpack_bf16_pair(x_ref[rows, :])
            accg[rows, :] += (jnp.dot(lo, wb[0], preferred_element_type=F32)
                              + jnp.dot(hi, wb[1], preferred_element_type=F32))
            accu[rows, :] += (jnp.dot(lo, wb[2], preferred_element_type=F32)
                              + jnp.dot(hi, wb[3], preferred_element_type=F32))

        _over_valid_rows(ns, accg.shape[0], sub, body)

        @pl.when(k == nk - 1)
        def _():
            g = accg[...]
            o_ref[...] = (g * _sigmoid(g) * accu[...]).astype(o_ref.dtype)


def _moe_down_kernel(nsub_ref, cx_ref, ce_ref, x_ref, w_ref, o_ref, wb, *, sub):
    del cx_ref, ce_ref
    c = pl.program_id(0)
    k = pl.program_id(2)
    ns = nsub_ref[c]

    @pl.when(ns > 0)
    def _():
        @pl.when(k == 0)
        def _():
            o_ref[...] = jnp.zeros_like(o_ref)

        wb[...] = w_ref[...].astype(BF16)

        def body(rows):
            o_ref[rows, :] += jnp.dot(x_ref[rows, :], wb[...], preferred_element_type=F32)

        _over_valid_rows(ns, o_ref.shape[0], sub, body)


def _moe_experts(xs, wgu, wdn, nsub, cx, ce, *, d, d_ff, rc, nch, sub):
    half = d // 2
    tkp = _pick(half, (512, 256, 128))
    nk = half // tkp
    tn = _pick(d_ff, (512, 256, 128))
    nj = d_ff // tn

    def frozen(c, a, last, nsub_r):
        return jnp.where(nsub_r[c] > 0, a, last)

    def x_map(c, j, k, nsub_r, cx_r, ce_r):
        return (cx_r[c], frozen(c, k, nk - 1, nsub_r))

    def w_map(col_off, row_off):
        def f(c, j, k, nsub_r, cx_r, ce_r):
            return (ce_r[c] * (d // tkp) + row_off // tkp + frozen(c, k, nk - 1, nsub_r),
                    col_off // tn + frozen(c, j, nj - 1, nsub_r))
        return f

    def o_map(c, j, k, nsub_r, cx_r, ce_r):
        return (cx_r[c], frozen(c, j, nj - 1, nsub_r))

    act = pl.pallas_call(
        functools.partial(_moe_gu_kernel, nk=nk, sub=sub),
        out_shape=jax.ShapeDtypeStruct((nch * rc, d_ff), BF16),
        grid_spec=pltpu.PrefetchScalarGridSpec(
            num_scalar_prefetch=3,
            grid=(nch, nj, nk),
            in_specs=[pl.BlockSpec((rc, tkp), x_map),
                      pl.BlockSpec((tkp, tn), w_map(0, 0)),
                      pl.BlockSpec((tkp, tn), w_map(0, half)),
                      pl.BlockSpec((tkp, tn), w_map(d_ff, 0)),
                      pl.BlockSpec((tkp, tn), w_map(d_ff, half))],
            out_specs=pl.BlockSpec((rc, tn), o_map),
            scratch_shapes=[pltpu.VMEM((rc, tn), F32), pltpu.VMEM((rc, tn), F32),
                            pltpu.VMEM((4, tkp, tn), BF16)]),
        compiler_params=_params(("arbitrary", "arbitrary", "arbitrary")),
        name="moe_gu",
    )(nsub, cx, ce, xs, wgu, wgu, wgu, wgu)

    tk2 = _pick(d_ff, (1024, 512, 256, 128))
    nk2 = d_ff // tk2
    tn2 = _pick(d, (1024, 512, 256, 128))
    nj2 = d // tn2

    def x2_map(c, j, k, nsub_r, cx_r, ce_r):
        return (cx_r[c], frozen(c, k, nk2 - 1, nsub_r))

    def w2_map(c, j, k, nsub_r, cx_r, ce_r):
        return (ce_r[c] * nk2 + frozen(c, k, nk2 - 1, nsub_r), frozen(c, j, nj2 - 1, nsub_r))

    def o2_map(c, j, k, nsub_r, cx_r, ce_r):
        return (cx_r[c], frozen(c, j, nj2 - 1, nsub_r))

    return pl.pallas_call(
        functools.partial(_moe_down_kernel, sub=sub),
        out_shape=jax.ShapeDtypeStruct((nch * rc, d), F32),
        grid_spec=pltpu.PrefetchScalarGridSpec(
            num_scalar_prefetch=3,
            grid=(nch, nj2, nk2),
            in_specs=[pl.BlockSpec((rc, tk2), x2_map),
                      pl.BlockSpec((tk2, tn2), w2_map)],
            out_specs=pl.BlockSpec((rc, tn2), o2_map),
            scratch_shapes=[pltpu.VMEM((tk2, tn2), BF16)]),
        compiler_params=_params(("arbitrary", "arbitrary", "arbitrary")),
        name="moe_down",
    )(nsub, cx, ce, act, wdn)


def _moe_combine_kernel(pos_ref, x_ref, wts_ref, y_ref, o_ref, ybuf, sems, *, tt):
    i = pl.program_id(0)
    del i

    def copies(r):
        return [pltpu.make_async_copy(y_ref.at[pl.ds(pos_ref[0, 0, TOP_K * r + s], 1), :],
                                      ybuf.at[s, pl.ds(r, 1), :], sems.at[s])
                for s in range(TOP_K)]

    def start(r, carry):
        for cp in copies(r):
            cp.start()
        return carry

    def wait(r, carry):
        for cp in copies(r):
            cp.wait()
        return carry

    lax.fori_loop(0, tt, start, 0)
    lax.fori_loop(0, tt, wait, 0)
    w = wts_ref[...]
    o_ref[...] = x_ref[...] + (w[:, 0:1] * ybuf[0] + w[:, 1:2] * ybuf[1])


def _moe_combine(x, wts, y, pos):
    m, d = x.shape
    tt = _pick(m, (192, 128, 64, 32, 16))
    pos3 = pos.reshape(m // tt, 1, TOP_K * tt)
    return pl.pallas_call(
        functools.partial(_moe_combine_kernel, tt=tt),
        out_shape=jax.ShapeDtypeStruct((m, d), F32),
        grid=(m // tt,),
        in_specs=[pl.BlockSpec((1, 1, TOP_K * tt), lambda i: (i, 0, 0), memory_space=pltpu.SMEM),
                  pl.BlockSpec((tt, d), lambda i: (i, 0)),
                  pl.BlockSpec((tt, LANES), lambda i: (i, 0)),
                  pl.BlockSpec(memory_space=pl.ANY)],
        out_specs=pl.BlockSpec((tt, d), lambda i: (i, 0)),
        scratch_shapes=[pltpu.VMEM((TOP_K, tt, d), F32), pltpu.SemaphoreType.DMA((TOP_K,))],
        compiler_params=_params(("arbitrary",)),
        name="moe_combine",
    )(pos3, x, wts, y)


def kernel(x_prompt, x_sample, state_conv, cache_kv, cache_win, page_table, norm_mix, norm_ffn, conv_w_in, conv_w, conv_w_out, ffn_w_gu, ffn_w_down, moe_router, moe_w_gu, moe_w_down, kv_norm, w_kv, k_norm, cmp_w1, cmp_w2, cmp_pe, w_qg, q_norm, w_o):
    batch, seq, d = x_prompt.shape
    dec_batch, dec_seq, _ = x_sample.shape
    n_pool, page_size = cache_kv.shape[:2]
    n_pages = page_table.shape[1]
    past_len = n_pages * page_size
    l_win = cache_win.shape[1]
    d_ff = ffn_w_down.shape[1]
    n_experts = moe_router.shape[2]
    n_heads = d // HEAD_DIM
    hpg = n_heads // N_KV
    sec_w = N_KV * HEAD_DIM
    m_prompt = batch * seq
    ms = dec_batch * dec_seq
    m = m_prompt + ms
    assert seq & (seq - 1) == 0 and dec_seq & (dec_seq - 1) == 0 and dec_seq >= CONV_W - 1
    assert m_prompt % ms == 0 and ms % 16 == 0 and seq % 128 == 0 and seq >= WINDOW + 128
    assert past_len % L_SLC == 0 and dec_seq <= L_SLC and page_size % L_SLC == 0
    assert norm_mix.shape[0] == 2 and l_win == WINDOW and past_len >= l_win

    x0 = jnp.concatenate([x_prompt.reshape(m_prompt, d), x_sample.reshape(ms, d)], axis=0)
    tm = _row_tile(m)

    (h0,) = _rmsnorm(x0, norm_mix[0:1])
    tn = _pick(d, (256, 128))
    tk = _pick(d, (1024, 512, 256, 128))
    b_gate, u = _matmul(
        h0, conv_w_in[0], col_offsets=(0, d, 2 * d), n_cols=d, tm=tm, tn=tn, tk=tk,
        epilogue=_epi_conv_in, out_shapes=[jax.ShapeDtypeStruct((m, d), F32)] * 2,
        out_specs=[_ij_spec(tm, tn)] * 2, name="conv_in")
    st = state_conv[0]
    zrow = jnp.zeros((dec_batch, dec_seq - 1, d), F32)
    s1 = jnp.concatenate([st[:, 1:2], zrow], axis=1).reshape(ms, d)
    s2 = jnp.concatenate([st[:, 0:1], st[:, 1:2], zrow[:, 1:]], axis=1).reshape(ms, d)
    z = _conv_gate(u, b_gate, conv_w[0], s1, s2, m_prompt=m_prompt, seq=seq, dec_seq=dec_seq)
    tn = _pick(d, (512, 256, 128))
    (x1,) = _matmul(
        z, conv_w_out[0], col_offsets=(0,), n_cols=d, tm=tm, tn=tn, tk=tk, epilogue=_epi_residual,
        extras=(x0,), extra_specs=(_ij_spec(tm, tn),), out_shapes=[jax.ShapeDtypeStruct((m, d), F32)],
        out_specs=[_ij_spec(tm, tn)], name="conv_out")
    u_p = u[:m_prompt].reshape(batch, seq, d)
    conv_prompt = u_p[:, seq - (CONV_W - 1):][None]
    conv_sample = u[m_prompt:].reshape(dec_batch, dec_seq, d)[:, dec_seq - (CONV_W - 1):][None]

    (h1,) = _rmsnorm(x1, norm_ffn[0:1])
    tnf = _pick(d_ff, (512, 256, 128))
    (act,) = _matmul(
        h1, ffn_w_gu[0], col_offsets=(0, d_ff), n_cols=d_ff, tm=tm, tn=tnf, tk=tk, epilogue=_epi_swiglu,
        out_shapes=[jax.ShapeDtypeStruct((m, d_ff), BF16)], out_specs=[_ij_spec(tm, tnf)], name="ffn_gu")
    tkf = _pick(d_ff, (1024, 512, 256, 128))
    (x2,) = _matmul(
        act, ffn_w_down[0], col_offsets=(0,), n_cols=d, tm=tm, tn=tn, tk=tkf, epilogue=_epi_residual,
        extras=(x1,), extra_specs=(_ij_spec(tm, tn),), out_shapes=[jax.ShapeDtypeStruct((m, d), F32)],
        out_specs=[_ij_spec(tm, tn)], name="ffn_down")

    hkv, h2 = _rmsnorm(x2, jnp.stack([kv_norm, norm_mix[1]]))
    ones = jnp.ones((HEAD_DIM,), F32)
    kv_gain = jnp.stack([ones, ones, k_norm[1], ones, k_norm[2], ones]).reshape(2 * N_BRANCH, 1, HEAD_DIM)
    kv_f32, kv_bf = _matmul(
        hkv, w_kv, col_offsets=(0,), n_cols=2 * N_BRANCH * sec_w, tm=tm, tn=sec_w, tk=tk, epilogue=_epi_kv,
        extras=(kv_gain,), extra_specs=(pl.BlockSpec((None, 1, HEAD_DIM), lambda i, j, k: (j, 0, 0)),),
        out_shapes=[jax.ShapeDtypeStruct((m, 2 * N_BRANCH * sec_w), F32),
                    jax.ShapeDtypeStruct((m, 2 * N_BRANCH * sec_w), BF16)],
        out_specs=[_ij_spec(tm, sec_w)] * 2, name="kv_proj")
    kv_p = kv_f32[:m_prompt].reshape(batch, seq, 2 * N_BRANCH, N_KV, HEAD_DIM)
    kv_s = kv_f32[m_prompt:].reshape(dec_batch, dec_seq, 2 * N_BRANCH, N_KV, HEAD_DIM)
    kv_prompt = kv_p[:, :, :4]
    win_prompt = kv_p[:, seq - min(WINDOW, seq):, 4:]
    kv_sample = kv_s[:, :, :4]
    win_sample = jnp.concatenate([cache_win, kv_s[:, :, 4:]], axis=1)[:, dec_seq:]

    wq = w_qg[0]
    (q,) = _matmul(
        h2, wq, col_offsets=(0,), n_cols=d, tm=tm, tn=sec_w, tk=tk, epilogue=_epi_q,
        extras=(q_norm[0:1],), extra_specs=(pl.BlockSpec((1, HEAD_DIM), lambda i, j, k: (0, 0)),),
        out_shapes=[jax.ShapeDtypeStruct((m, d), BF16)], out_specs=[_ij_spec(tm, sec_w)], name="q_proj")
    wg = wq[:, d:].reshape(d, N_KV, hpg, N_BRANCH).transpose(0, 1, 3, 2).reshape(d, N_KV, N_BRANCH * hpg)
    wg = jnp.pad(wg, ((0, 0), (0, 0), (0, LANES - N_BRANCH * hpg))).reshape(d, N_KV * LANES)
    (gates,) = _matmul(
        h2, wg, col_offsets=(0,), n_cols=N_KV * LANES, tm=tm, tn=N_KV * LANES, tk=tk, epilogue=_epi_gate,
        out_shapes=[jax.ShapeDtypeStruct((m, N_KV * LANES), F32)], out_specs=[_ij_spec(tm, N_KV * LANES)],
        name="gate_proj")

    w1r = cmp_w1.reshape(2, R_CMP, STRIDE * HEAD_DIM, HEAD_DIM)
    per = cmp_pe.reshape(2, R_CMP, 1, STRIDE * HEAD_DIM)
    kn0 = k_norm[0:1]
    kc_p = _compress_prompt(kv_f32, w1r, per, cmp_w2, kn0, batch=batch, seq=seq)
    pt_flat = page_table.reshape(-1)
    kc_s = _compress_sample(cache_kv, pt_flat, w1r, per, cmp_w2, kn0, dec_batch=dec_batch, n_pages=n_pages,
                            page_size=page_size)

    o = _nsa_prompt(q, gates, kc_p, kv_bf, m_total=m, batch=batch, seq=seq, hpg=hpg)
    win2d = cache_win.reshape(dec_batch * l_win, 2 * sec_w)
    o = _nsa_sample(q, gates, kc_s, kv_f32, cache_kv, win2d, pt_flat, o, m_prompt=m_prompt,
                    dec_batch=dec_batch, dec_seq=dec_seq, hpg=hpg, past_len=past_len, n_pages=n_pages,
                    page_size=page_size, l_win=l_win)
    (x3,) = _matmul(
        o, w_o[0], col_offsets=(0,), n_cols=d, tm=tm, tn=tn, tk=tk, epilogue=_epi_residual,
        extras=(x2,), extra_specs=(_ij_spec(tm, tn),), out_shapes=[jax.ShapeDtypeStruct((m, d), F32)],
        out_specs=[_ij_spec(tm, tn)], name="attn_out")

    router_p = jnp.pad(moe_router[0], ((0, 0), (0, LANES - n_experts)))
    hp, ridx, wts = _router(x3, norm_ffn[1:2], router_p, n_experts)
    wgu = moe_w_gu[0].reshape(n_experts * d, 2 * d_ff)
    wdn = moe_w_down[0].reshape(n_experts * d_ff, d)
    sub = MOE_SUB_ROWS
    rc = -(-(TOP_K * m * 12 // (10 * n_experts)) // (2 * sub)) * (2 * sub)
    nch = TOP_K * m // rc + n_experts
    pos, nsub, cx, ce = _moe_plan(ridx[:, :TOP_K], n_experts, rc, nch, sub)
    xs = _moe_scatter(hp, pos, nch * rc)
    y = _moe_experts(xs, wgu, wdn, nsub, cx, ce, d=d, d_ff=d_ff, rc=rc, nch=nch, sub=sub)
    x4 = _moe_combine(x3, wts, y, pos)

    y_prompt = x4[:m_prompt].reshape(batch, seq, d)
    y_sample = x4[m_prompt:].reshape(dec_batch, dec_seq, d)
    return (y_prompt, y_sample, conv_prompt, kv_prompt, win_prompt, conv_sample, kv_sample, win_sample)
```

```python
import functools

import jax
import jax.numpy as jnp
from jax import lax
from jax.experimental import pallas as pl
from jax.experimental.pallas import tpu as pltpu

HEAD_DIM = 128
N_KV = 4
N_BRANCH = 3
L_CMP = 32
STRIDE = 16
R_CMP = L_CMP // STRIDE
L_SLC = 64
N_SEL = 16
WINDOW = 512
TOP_K = 2
CONV_W = 3
EPS = 1e-6
NEG = -1e30
FORCE = 1e6

LANES = 128
MOE_SUB_ROWS = 128
FLASH_SLAB_ROWS = 64
M_FLOOR = -1e29
LOG2_E = 1.4426950408889634
VMEM_LIMIT = 56 * 1024 * 1024

F32 = jnp.float32
BF16 = jnp.bfloat16


def _pick(n, candidates):
    for c in candidates:
        if c <= n and n % c == 0:
            return c
    return n


def _params(sem):
    return pltpu.CompilerParams(dimension_semantics=sem, vmem_limit_bytes=VMEM_LIMIT)


def _sigmoid(x):
    return 0.5 * jnp.tanh(0.5 * x) + 0.5


def _div_pow2(x, n):
    assert n & (n - 1) == 0
    return jnp.right_shift(x, n.bit_length() - 1)


def _rmsnorm_kernel(x_ref, g_ref, *o_refs):
    x = x_ref[...]
    y = x * lax.rsqrt(jnp.mean(x * x, axis=-1, keepdims=True) + EPS)
    for n, o_ref in enumerate(o_refs):
        o_ref[...] = (y * g_ref[n:n + 1, :]).astype(o_ref.dtype)


def _rmsnorm(x, gains):
    m, d = x.shape
    n = gains.shape[0]
    tm = _pick(m, (192, 128, 64, 32, 16))
    outs = pl.pallas_call(
        _rmsnorm_kernel,
        out_shape=[jax.ShapeDtypeStruct((m, d), BF16)] * n,
        grid=(m // tm,),
        in_specs=[pl.BlockSpec((tm, d), lambda i: (i, 0)),
                  pl.BlockSpec((n, d), lambda i: (0, 0))],
        out_specs=[pl.BlockSpec((tm, d), lambda i: (i, 0))] * n,
        compiler_params=_params(("parallel",)),
        name="rmsnorm",
    )(x, gains)
    return outs


def _mm_kernel(*refs, nk, n_w, n_extra, n_out, epilogue):
    x_ref = refs[0]
    w_refs = refs[1:1 + n_w]
    extra = refs[1 + n_w:1 + n_w + n_extra]
    out_refs = refs[1 + n_w + n_extra:1 + n_w + n_extra + n_out]
    acc_refs = refs[1 + n_w + n_extra + n_out:]
    i = pl.program_id(0)
    j = pl.program_id(1)
    k = pl.program_id(2)

    def step(first, last):
        x = x_ref[...]
        vals = []
        for w_ref, a in zip(w_refs, acc_refs):
            part = jnp.dot(x, w_ref[...].astype(BF16), preferred_element_type=F32)
            vals.append(part if first else a[...] + part)
        if last:
            epilogue((i, j), vals, extra, out_refs)
        else:
            for a, v in zip(acc_refs, vals):
                a[...] = v

    if nk == 1:
        step(True, True)
    else:
        pl.when(k == 0)(lambda: step(True, False))
        if nk > 2:
            pl.when((k > 0) & (k < nk - 1))(lambda: step(False, False))
        pl.when(k == nk - 1)(lambda: step(False, True))


def _matmul(x, w, *, col_offsets, n_cols, tm, tn, tk, epilogue, extras=(), extra_specs=(),
            out_shapes, out_specs, w_row_offset=0, name):
    m, kdim = x.shape
    nk = kdim // tk
    n_w = len(col_offsets)
    in_specs = [pl.BlockSpec((tm, tk), lambda i, j, k: (i, k))]
    for off in col_offsets:
        in_specs.append(pl.BlockSpec(
            (tk, tn), lambda i, j, k, off=off: (k + w_row_offset // tk, j + off // tn)))
    in_specs += list(extra_specs)
    kern = functools.partial(_mm_kernel, nk=nk, n_w=n_w, n_extra=len(extras),
                             n_out=len(out_shapes), epilogue=epilogue)
    return pl.pallas_call(
        kern,
        out_shape=out_shapes,
        grid=(m // tm, n_cols // tn, nk),
        in_specs=in_specs,
        out_specs=out_specs,
        scratch_shapes=[pltpu.VMEM((tm, tn), F32)] * n_w,
        compiler_params=_params(("parallel", "parallel", "arbitrary")),
        name=name,
    )(x, *([w] * n_w), *extras)


def _row_tile(m):
    return _pick(m, (2064, 2048, 1024, 688, 512, 256, 192, 128, 64, 32, 16))


def _ij_spec(tm, tn):
    return pl.BlockSpec((tm, tn), lambda i, j, k: (i, j))


def _epi_conv_in(ids, accs, extra, outs):
    b, c, v = accs
    outs[0][...] = b
    outs[1][...] = c * v


def _epi_residual(ids, accs, extra, outs):
    outs[0][...] = extra[0][...] + accs[0]


def _epi_swiglu(ids, accs, extra, outs):
    g, u = accs
    outs[0][...] = (g * _sigmoid(g) * u).astype(outs[0].dtype)


def _head_rmsnorm(a, gain):
    parts = []
    for h in range(a.shape[1] // HEAD_DIM):
        ah = a[:, h * HEAD_DIM:(h + 1) * HEAD_DIM]
        ms = jnp.mean(ah * ah, axis=-1, keepdims=True)
        parts.append(ah * lax.rsqrt(ms + EPS) * gain)
    return jnp.concatenate(parts, axis=1)


def _epi_kv(ids, accs, extra, outs):
    _, j = ids
    a = accs[0]
    normed = _head_rmsnorm(a, extra[0][...])
    y = jnp.where((j == 2) | (j == 4), normed, a)
    outs[0][...] = y
    outs[1][...] = y.astype(BF16)


def _epi_q(ids, accs, extra, outs):
    outs[0][...] = _head_rmsnorm(accs[0], extra[0][...]).astype(BF16)


def _epi_gate(ids, accs, extra, outs):
    outs[0][...] = _sigmoid(accs[0])


def _conv_kernel(u_ref, up_ref, b_ref, cw_ref, s1_ref, s2_ref, z_ref, *, tm, m_prompt, seq, dec_seq,
                 n_row_tiles):
    i = pl.program_id(0)
    u = u_ref[...]
    prev = up_ref[...]
    loc = lax.broadcasted_iota(jnp.int32, (tm, 1), 0)
    r = i * tm + loc
    u1 = pltpu.roll(u, 1, 0)
    u1 = jnp.where(loc == 0, prev[7:8, :], u1)
    u2 = pltpu.roll(u, 2, 0)
    u2 = jnp.where(loc == 0, prev[6:7, :], jnp.where(loc == 1, prev[7:8, :], u2))
    t = jnp.where(r < m_prompt, r & (seq - 1), (r - m_prompt) & (dec_seq - 1))
    u1 = jnp.where(t >= 1, u1, 0.0)
    u2 = jnp.where(t >= 2, u2, 0.0)
    w0 = cw_ref[0:1, :]
    w1 = cw_ref[1:2, :]
    w2 = cw_ref[2:3, :]
    conv = w2 * u + w1 * u1 + w0 * u2
    z_ref[...] = (b_ref[...] * conv).astype(z_ref.dtype)
    ms = s1_ref.shape[0]

    @pl.when(i == n_row_tiles - 1)
    def _():
        tail = conv[tm - ms:, :] + w1 * s1_ref[...] + w0 * s2_ref[...]
        z_ref[tm - ms:, :] = (b_ref[tm - ms:, :] * tail).astype(z_ref.dtype)


def _conv_gate(u, b, cw, s1, s2, *, m_prompt, seq, dec_seq):
    m, d = u.shape
    ms = s1.shape[0]
    tm = _row_tile(m)
    tc = _pick(d, (512, 256, 128))
    n_row_tiles = m // tm
    kern = functools.partial(_conv_kernel, tm=tm, m_prompt=m_prompt, seq=seq, dec_seq=dec_seq,
                             n_row_tiles=n_row_tiles)
    return pl.pallas_call(
        kern,
        out_shape=jax.ShapeDtypeStruct((m, d), BF16),
        grid=(n_row_tiles, d // tc),
        in_specs=[pl.BlockSpec((tm, tc), lambda i, j: (i, j)),
                  pl.BlockSpec((8, tc), lambda i, j: (jnp.maximum(i * (tm // 8) - 1, 0), j)),
                  pl.BlockSpec((tm, tc), lambda i, j: (i, j)),
                  pl.BlockSpec((CONV_W, tc), lambda i, j: (0, j)),
                  pl.BlockSpec((ms, tc), lambda i, j: (0, j)),
                  pl.BlockSpec((ms, tc), lambda i, j: (0, j))],
        out_specs=pl.BlockSpec((tm, tc), lambda i, j: (i, j)),
        compiler_params=_params(("parallel", "parallel")),
        name="conv_gate",
    )(u, u, b, cw, s1, s2)


def _cmp_stage1_compute(head_refs, w1_ref, pe_ref, top_ref, bot_ref, nch):
    for sec in range(2):
        wt = w1_ref[sec, 0].astype(BF16)
        wb = w1_ref[sec, 1].astype(BF16)
        pt = pe_ref[sec, 0]
        pb = pe_ref[sec, 1]
        rows = []
        for g in range(N_KV):
            ref = head_refs[sec * N_KV + g]
            cols = [ref[pl.ds(s, nch, stride=STRIDE), :] for s in range(STRIDE)]
            rows.append(jnp.concatenate(cols, axis=1))
        a = jnp.concatenate(rows, axis=0)
        top = jnp.dot((a + pt).astype(BF16), wt, preferred_element_type=F32)
        bot = jnp.dot((a + pb).astype(BF16), wb, preferred_element_type=F32)
        for g in range(N_KV):
            hh = sec * N_KV + g
            top_ref[:, hh * HEAD_DIM:(hh + 1) * HEAD_DIM] = top[g * nch:(g + 1) * nch, :]
            bot_ref[:, hh * HEAD_DIM:(hh + 1) * HEAD_DIM] = bot[g * nch:(g + 1) * nch, :]


def _cmp1_prompt_kernel(*refs, nch):
    n_heads = 2 * N_KV
    head_refs = refs[:n_heads]
    w1_ref, pe_ref, top_ref, bot_ref = refs[n_heads:]
    _cmp_stage1_compute(head_refs, w1_ref, pe_ref, top_ref, bot_ref, nch)


def _cmp1_sample_kernel(pt_ref, cache_ref, w1_ref, pe_ref, top_ref, bot_ref, buf_ref, sem, *,
                        pages_per_step, n_pages, page_size, n_groups, n_steps):
    step = pl.program_id(0) * n_groups + pl.program_id(1)
    slot = step % 2
    n_heads = 2 * N_KV

    def copies(st, sl):
        first = (st // n_groups) * n_pages + (st % n_groups) * pages_per_step
        out = []
        for p in range(pages_per_step):
            page = pt_ref[first + p]
            for hh in range(n_heads):
                out.append(pltpu.make_async_copy(
                    cache_ref.at[page, :, hh // N_KV, hh % N_KV, :],
                    buf_ref.at[sl, hh, pl.ds(p * page_size, page_size), :],
                    sem.at[sl]))
        return out

    @pl.when(step == 0)
    def _():
        for cp in copies(0, 0):
            cp.start()

    @pl.when(step + 1 < n_steps)
    def _():
        for cp in copies(step + 1, 1 - slot):
            cp.start()

    for cp in copies(step, slot):
        cp.wait()
    _cmp_stage1_compute([buf_ref.at[slot, hh] for hh in range(n_heads)], w1_ref, pe_ref, top_ref, bot_ref,
                        pages_per_step * page_size // STRIDE)


def _cmp2_kernel(top_ref, bot_ref, w2_ref, kn_ref, o_ref, *, nch):
    pre = top_ref[...] + pltpu.roll(bot_ref[...], nch - 1, 0)
    a = (pre * _sigmoid(pre)).astype(BF16)
    for sec in range(2):
        w2 = w2_ref[sec].astype(BF16)
        for g in range(N_KV):
            hh = sec * N_KV + g
            y = jnp.dot(a[:, hh * HEAD_DIM:(hh + 1) * HEAD_DIM], w2, preferred_element_type=F32)
            if sec == 0:
                y = _head_rmsnorm(y, kn_ref[...])
            o_ref[:, hh * HEAD_DIM:(hh + 1) * HEAD_DIM] = y.astype(o_ref.dtype)


def _cmp_stage2(top, bot, w2, kn0, nch):
    rows, width = top.shape
    return pl.pallas_call(
        functools.partial(_cmp2_kernel, nch=nch),
        out_shape=jax.ShapeDtypeStruct((rows, width), BF16),
        grid=(rows // nch,),
        in_specs=[pl.BlockSpec((nch, width), lambda b: (b, 0)),
                  pl.BlockSpec((nch, width), lambda b: (b, 0)),
                  pl.BlockSpec(w2.shape, lambda b: (0, 0, 0)),
                  pl.BlockSpec(kn0.shape, lambda b: (0, 0))],
        out_specs=pl.BlockSpec((nch, width), lambda b: (b, 0)),
        compiler_params=_params(("parallel",)),
        name="cmp_stage2",
    )(top, bot, w2, kn0)


def _compress_prompt(kv_f32, w1r, per, w2, kn0, *, batch, seq):
    width = 2 * N_KV * HEAD_DIM
    nch = seq // STRIDE
    top, bot = pl.pallas_call(
        functools.partial(_cmp1_prompt_kernel, nch=nch),
        out_shape=[jax.ShapeDtypeStruct((batch * nch, width), F32)] * 2,
        grid=(batch,),
        in_specs=[pl.BlockSpec((seq, HEAD_DIM), lambda b, hh=hh: (b, hh)) for hh in range(2 * N_KV)]
        + [pl.BlockSpec(w1r.shape, lambda b: (0, 0, 0, 0)),
           pl.BlockSpec(per.shape, lambda b: (0, 0, 0, 0))],
        out_specs=[pl.BlockSpec((nch, width), lambda b: (b, 0))] * 2,
        compiler_params=_params(("parallel",)),
        name="cmp_stage1_prompt",
    )(*([kv_f32] * (2 * N_KV)), w1r, per)
    return _cmp_stage2(top, bot, w2, kn0, nch)


def _compress_sample(cache2d, pt_flat, w1r, per, w2, kn0, *, dec_batch, n_pages, page_size):
    width = 2 * N_KV * HEAD_DIM
    pages_per_step = _pick(n_pages, (16, 8, 4, 2, 1))
    n_groups = n_pages // pages_per_step
    rows_step = pages_per_step * page_size
    nch_step = rows_step // STRIDE
    nch = n_pages * page_size // STRIDE
    kern = functools.partial(_cmp1_sample_kernel, pages_per_step=pages_per_step, n_pages=n_pages,
                             page_size=page_size, n_groups=n_groups, n_steps=dec_batch * n_groups)
    top, bot = pl.pallas_call(
        kern,
        out_shape=[jax.ShapeDtypeStruct((dec_batch * nch, width), F32)] * 2,
        grid_spec=pltpu.PrefetchScalarGridSpec(
            num_scalar_prefetch=1,
            grid=(dec_batch, n_groups),
            in_specs=[pl.BlockSpec(memory_space=pl.ANY),
                      pl.BlockSpec(w1r.shape, lambda b, g, pt: (0, 0, 0, 0)),
                      pl.BlockSpec(per.shape, lambda b, g, pt: (0, 0, 0, 0))],
            out_specs=[pl.BlockSpec((nch_step, width), lambda b, g, pt: (b * n_groups + g, 0))] * 2,
            scratch_shapes=[pltpu.VMEM((2, 2 * N_KV, rows_step, HEAD_DIM), F32),
                            pltpu.SemaphoreType.DMA((2,))]),
        compiler_params=_params(("arbitrary", "arbitrary")),
        name="cmp_stage1_sample",
    )(pt_flat, cache2d, w1r, per)
    return _cmp_stage2(top, bot, w2, kn0, nch)


def _split3(x):
    hi = x.astype(BF16)
    r1 = x - hi.astype(F32)
    mid = r1.astype(BF16)
    lo = (r1 - mid.astype(F32)).astype(BF16)
    return hi, mid, lo


def _dot_f32ish(x, m_bf16):
    hi, mid, lo = _split3(x)
    out = jnp.dot(lo, m_bf16, preferred_element_type=F32)
    out = out + jnp.dot(mid, m_bf16, preferred_element_type=F32)
    return out + jnp.dot(hi, m_bf16, preferred_element_type=F32)


def _inter_t(n_c, n_s_lanes):
    c0 = lax.broadcasted_iota(jnp.int32, (n_c, n_s_lanes), 0) * STRIDE
    s0 = lax.broadcasted_iota(jnp.int32, (n_c, n_s_lanes), 1) * L_SLC
    return ((c0 < s0 + L_SLC) & (c0 + L_CMP > s0)).astype(BF16)


def _rank_desc(score, ns):
    lane = lax.broadcasted_iota(jnp.int32, score.shape, 1)
    rank = jnp.zeros(score.shape, jnp.int32)
    for sp in range(ns):
        col = score[:, sp:sp + 1]
        ahead = (col > score) | ((col == score) & (lane > sp))
        rank = rank + ahead.astype(jnp.int32)
    return rank


def _selection_scores(imp, qblk, ns):
    j = lax.broadcasted_iota(jnp.int32, imp.shape, 1)
    forced = (j == 0) | (j == qblk) | (j == qblk - 1)
    score = jnp.where(forced, FORCE, imp)
    score = jnp.where(j <= qblk, score, -FORCE)
    return jnp.where(j < ns, score, -2.0 * FORCE)


def _flash_init(m_sc, l_sc, acc_sc):
    m_sc[...] = jnp.full(m_sc.shape, M_FLOOR, F32)
    l_sc[...] = jnp.zeros(l_sc.shape, F32)
    acc_sc[...] = jnp.zeros(acc_sc.shape, F32)


def _flash_tile(q, kt, vt, *, s_sc, p_sc, bias_sc, m_sc, l_sc, alpha_sc, acc_sc, tq, width):
    c = (HEAD_DIM ** -0.5) * LOG2_E
    nt = (((1,), (1,)), ((), ()))
    reps = width // LANES
    s_sc[:, :width] = lax.dot_general(q, kt, nt, preferred_element_type=F32)
    slabs = [(slice(r0, r0 + FLASH_SLAB_ROWS), slice(r0 % tq, r0 % tq + FLASH_SLAB_ROWS))
             for r0 in range(0, q.shape[0], FLASH_SLAB_ROWS)]
    for rs, qs in slabs:
        s = s_sc[rs, :width] * c + bias_sc[qs, :width]
        s_sc[rs, :width] = s
        m_old = m_sc[rs, :]
        m_new = jnp.maximum(m_old, jnp.max(s, axis=-1, keepdims=True))
        alpha_sc[rs, :] = jnp.exp2(m_old - m_new)
        m_sc[rs, :] = m_new
    for rs, qs in slabs:
        m_b = jnp.concatenate([m_sc[rs, :]] * reps, axis=1)
        p = jnp.exp2(s_sc[rs, :width] - m_b)
        l_sc[rs, :] = alpha_sc[rs, :] * l_sc[rs, :] + jnp.sum(p, axis=-1, keepdims=True)
        p_sc[rs, :width] = p.astype(p_sc.dtype)
    pv = jnp.dot(p_sc[:, :width], vt, preferred_element_type=F32)
    acc_sc[...] = alpha_sc[...] * acc_sc[...] + pv


def _nsa_prompt_kernel(q_ref, g_ref, kc_ref, vc_ref, ks_ref, vs_ref, kw_ref, vw_ref, o_init_ref, o_ref,
                       s_sc, p_sc, bias_sc, m_sc, l_sc, alpha_sc, acc_sc, *, tq, tk, hpg, seq, ncp):
    del o_init_ref
    qt = pl.program_id(2)
    t0 = qt * tq
    scale = HEAD_DIM ** -0.5
    rows = hpg * tq
    qf = q_ref[...]
    q = jnp.concatenate([qf[:, h * HEAD_DIM:(h + 1) * HEAD_DIM] for h in range(hpg)], axis=0)
    qpos = t0 + lax.broadcasted_iota(jnp.int32, (tq, 1), 0)
    nt = (((1,), (1,)), ((), ()))

    s = lax.dot_general(q, kc_ref[...], nt, preferred_element_type=F32) * scale
    c_end = lax.broadcasted_iota(jnp.int32, (tq, ncp), 1) * STRIDE + (L_CMP - 1)
    mc = (c_end <= qpos)[None]
    s3 = jnp.where(mc, s.reshape(hpg, tq, ncp), NEG)
    e = jnp.where(mc, jnp.exp(s3 - jnp.max(s3, axis=-1, keepdims=True)), 0.0)
    den = jnp.sum(e, axis=-1, keepdims=True)
    p_c = jnp.where(den > 0.0, e / jnp.where(den > 0.0, den, 1.0), 0.0)
    o_c = jnp.dot(p_c.reshape(rows, ncp).astype(BF16), vc_ref[...], preferred_element_type=F32)
    psum = jnp.sum(p_c, axis=0)

    ns = -(-seq // L_SLC)
    imp = _dot_f32ish(psum, _inter_t(ncp, LANES))
    qblk = _div_pow2(qpos, L_SLC)
    score = _selection_scores(imp, qblk, ns)
    rank = _rank_desc(score, ns)
    sel = ((rank < N_SEL) & (score > -0.5 * FORCE)).astype(BF16)

    flash = functools.partial(_flash_tile, s_sc=s_sc, p_sc=p_sc, bias_sc=bias_sc, m_sc=m_sc, l_sc=l_sc,
                              alpha_sc=alpha_sc, acc_sc=acc_sc, tq=tq)
    _flash_init(m_sc, l_sc, acc_sc)

    def body(j, carry):
        k0 = pl.multiple_of(j * tk, tk)
        kpos = k0 + lax.broadcasted_iota(jnp.int32, (1, tk), 1)
        expand = (lax.broadcasted_iota(jnp.int32, (LANES, tk), 0) == _div_pow2(kpos, L_SLC)).astype(BF16)
        km = jnp.dot(sel, expand, preferred_element_type=F32) > 0.5
        bias_sc[:, :tk] = jnp.where(km & (kpos <= qpos), 0.0, NEG)
        flash(q, ks_ref[pl.ds(k0, tk), :], vs_ref[pl.ds(k0, tk), :], width=tk)
        return carry

    lax.fori_loop(0, (t0 + tq - 1) // tk + 1, body, 0)
    o_s = acc_sc[...] / l_sc[...]

    wl = WINDOW + tq
    w0 = pl.multiple_of(jnp.clip(t0 - WINDOW, 0, seq - wl), tq)
    wpos = w0 + lax.broadcasted_iota(jnp.int32, (1, wl), 1)
    dlt = qpos - wpos
    bias_sc[...] = jnp.where((dlt >= 0) & (dlt <= WINDOW), 0.0, NEG)
    _flash_init(m_sc, l_sc, acc_sc)
    flash(q, kw_ref[pl.ds(w0, wl), :], vw_ref[pl.ds(w0, wl), :], width=wl)
    o_w = acc_sc[...] / l_sc[...]

    gt = g_ref[...]
    for h in range(hpg):
        sl = slice(h * tq, (h + 1) * tq)
        g0 = gt[:, 0 * hpg + h:0 * hpg + h + 1]
        g1 = gt[:, 1 * hpg + h:1 * hpg + h + 1]
        g2 = gt[:, 2 * hpg + h:2 * hpg + h + 1]
        o_ref[:, h * HEAD_DIM:(h + 1) * HEAD_DIM] = (
            g0 * o_c[sl, :] + g1 * o_s[sl, :] + g2 * o_w[sl, :]).astype(o_ref.dtype)


def _nsa_prompt(q, gates, kc, kv_bf, *, m_total, batch, seq, hpg):
    d = q.shape[1]
    tq = 128
    tk = _pick(seq, (512, 256, 128))
    ncp = seq // STRIDE
    nqt = seq // tq
    gw = hpg * HEAD_DIM
    rows = hpg * tq
    wmax = max(tk, WINDOW + tq)
    kern = functools.partial(_nsa_prompt_kernel, tq=tq, tk=tk, hpg=hpg, seq=seq, ncp=ncp)

    def kvspec(sec):
        return pl.BlockSpec((seq, HEAD_DIM), lambda b, g, t, sec=sec: (b, sec * N_KV + g))

    return pl.pallas_call(
        kern,
        out_shape=jax.ShapeDtypeStruct((m_total, d), BF16),
        grid=(batch, N_KV, nqt),
        in_specs=[pl.BlockSpec((tq, gw), lambda b, g, t: (b * nqt + t, g)),
                  pl.BlockSpec((tq, LANES), lambda b, g, t: (b * nqt + t, g)),
                  pl.BlockSpec((ncp, HEAD_DIM), lambda b, g, t: (b, g)),
                  pl.BlockSpec((ncp, HEAD_DIM), lambda b, g, t: (b, N_KV + g)),
                  kvspec(2), kvspec(3), kvspec(4), kvspec(5),
                  pl.BlockSpec(memory_space=pl.ANY)],
        out_specs=pl.BlockSpec((tq, gw), lambda b, g, t: (b * nqt + t, g)),
        scratch_shapes=[pltpu.VMEM((rows, wmax), F32),
                        pltpu.VMEM((rows, wmax), BF16),
                        pltpu.VMEM((tq, wmax), F32),
                        pltpu.VMEM((rows, LANES), F32), pltpu.VMEM((rows, LANES), F32),
                        pltpu.VMEM((rows, LANES), F32), pltpu.VMEM((rows, HEAD_DIM), F32)],
        input_output_aliases={8: 0},
        compiler_params=_params(("parallel", "parallel", "arbitrary")),
        name="nsa_prompt",
    )(q, gates, kc, kc, kv_bf, kv_bf, kv_bf, kv_bf, jnp.zeros((m_total, d), BF16))


def _nsa_sample_select_kernel(q_ref, kc_ref, vc_ref, oc_ref, idx_ref, *, dec_batch, dec_seq, hpg,
                              past_len, ncp, ns, ns_lanes):
    scale = HEAD_DIM ** -0.5
    nt = (((1,), (1,)), ((), ()))
    qf = q_ref[...].astype(F32)
    rows = hpg * dec_seq
    qrow = lax.broadcasted_iota(jnp.int32, (dec_seq, 1), 0)
    qpos = past_len + qrow
    c_end = lax.broadcasted_iota(jnp.int32, (dec_seq, ncp), 1) * STRIDE + (L_CMP - 1)
    mc = (c_end <= qpos)[None]
    psums = []
    for b in range(dec_batch):
        qb = qf[b * dec_seq:(b + 1) * dec_seq, :]
        q = jnp.concatenate([qb[:, h * HEAD_DIM:(h + 1) * HEAD_DIM] for h in range(hpg)],
                            axis=0).astype(BF16)
        kc = kc_ref[b * ncp:(b + 1) * ncp, :]
        vc = vc_ref[b * ncp:(b + 1) * ncp, :]
        s = lax.dot_general(q, kc, nt, preferred_element_type=F32) * scale
        s3 = jnp.where(mc, s.reshape(hpg, dec_seq, ncp), NEG)
        e = jnp.where(mc, jnp.exp(s3 - jnp.max(s3, axis=-1, keepdims=True)), 0.0)
        den = jnp.sum(e, axis=-1, keepdims=True)
        p_c = jnp.where(den > 0.0, e / jnp.where(den > 0.0, den, 1.0), 0.0)
        o_c = jnp.dot(p_c.reshape(rows, ncp).astype(BF16), vc, preferred_element_type=F32)
        for h in range(hpg):
            oc_ref[b * dec_seq:(b + 1) * dec_seq, h * HEAD_DIM:(h + 1) * HEAD_DIM] = (
                o_c[h * dec_seq:(h + 1) * dec_seq, :])
        psums.append(jnp.sum(p_c, axis=0))
    psum = jnp.concatenate(psums, axis=0)
    imp = _dot_f32ish(psum, _inter_t(ncp, ns_lanes))
    n_rows = dec_batch * dec_seq
    qpos_all = past_len + (lax.broadcasted_iota(jnp.int32, (n_rows, 1), 0) & (dec_seq - 1))
    score = _selection_scores(imp, _div_pow2(qpos_all, L_SLC), ns)
    rank = _rank_desc(score, ns)
    ok = score > -0.5 * FORCE
    lane = lax.broadcasted_iota(jnp.int32, score.shape, 1)
    out_lane = lax.broadcasted_iota(jnp.int32, (n_rows, LANES), 1)
    idx = jnp.full((n_rows, LANES), -1, jnp.int32)
    for k in range(N_SEL):
        hit = (rank == k) & ok
        blk = jnp.sum(jnp.where(hit, (lane + 1).astype(F32), 0.0), axis=-1, keepdims=True)
        blk = blk.astype(jnp.int32) - 1
        idx = jnp.where(out_lane == k, blk, idx)
    idx_ref[...] = idx


def _nsa_sample_attend_kernel(idx_ref, pt_ref, q_ref, g_ref, oc_ref, knew_ref, vnew_ref, kwo_ref, vwo_ref,
                              kwn_ref, vwn_ref, cache_ref, o_in_ref, o_ref, kbuf, vbuf, of32, sems, *,
                              dec_batch, dec_seq, hpg, past_len, n_pages, page_size, l_win):
    del o_in_ref
    g = pl.program_id(0)
    scale = HEAD_DIM ** -0.5
    nt = (((1,), (1,)), ((), ()))
    rows = hpg * dec_seq
    n_past_blocks = past_len // L_SLC
    blocks_per_page = page_size // L_SLC
    nkeys = N_SEL * L_SLC
    qf = q_ref[...].astype(F32)
    gt = g_ref[...]
    qrow = lax.broadcasted_iota(jnp.int32, (rows, 1), 0) & (dec_seq - 1)
    key_slot = _div_pow2(lax.broadcasted_iota(jnp.int32, (1, nkeys), 1), L_SLC)
    newj = lax.broadcasted_iota(jnp.int32, (1, dec_seq), 1)

    for b in range(dec_batch):
        def copies(qi, k):
            blk = idx_ref[((g * dec_batch + b) * dec_seq + qi) * N_SEL + k]
            blk = jnp.clip(blk, 0, n_past_blocks - 1)
            page = pt_ref[b * n_pages + blk // blocks_per_page]
            row0 = pl.multiple_of((blk % blocks_per_page) * L_SLC, L_SLC)
            ck = pltpu.make_async_copy(cache_ref.at[page, pl.ds(row0, L_SLC), 2, g, :],
                                       kbuf.at[qi, pl.ds(k * L_SLC, L_SLC), :], sems.at[0])
            cv = pltpu.make_async_copy(cache_ref.at[page, pl.ds(row0, L_SLC), 3, g, :],
                                       vbuf.at[qi, pl.ds(k * L_SLC, L_SLC), :], sems.at[1])
            return ck, cv

        for qi in range(dec_seq):
            for k in range(N_SEL):
                ck, cv = copies(qi, k)
                ck.start()
                cv.start()
        for qi in range(dec_seq):
            for k in range(N_SEL):
                ck, cv = copies(qi, k)
                ck.wait()
                cv.wait()

        qb = qf[b * dec_seq:(b + 1) * dec_seq, :]
        q = jnp.concatenate([qb[:, h * HEAD_DIM:(h + 1) * HEAD_DIM] for h in range(hpg)],
                            axis=0).astype(BF16)
        knew = knew_ref[b * dec_seq:(b + 1) * dec_seq, :].astype(BF16)
        vnew = vnew_ref[b * dec_seq:(b + 1) * dec_seq, :].astype(BF16)
        s_new = lax.dot_general(q, knew, nt, preferred_element_type=F32) * scale

        o_s = jnp.zeros((rows, HEAD_DIM), F32)
        for qi in range(dec_seq):
            valid = jnp.zeros((1, nkeys), jnp.int32)
            has_new = jnp.int32(0)
            for k in range(N_SEL):
                blk = idx_ref[((g * dec_batch + b) * dec_seq + qi) * N_SEL + k]
                is_past = ((blk >= 0) & (blk < n_past_blocks)).astype(jnp.int32)
                valid = jnp.where(key_slot == k, is_past, valid)
                has_new = has_new | (blk == n_past_blocks).astype(jnp.int32)
            mk = valid > 0
            mn = ((newj <= qi).astype(jnp.int32) * has_new) > 0
            kq = kbuf[qi].astype(BF16)
            vq = vbuf[qi].astype(BF16)
            s_old = lax.dot_general(q, kq, nt, preferred_element_type=F32) * scale
            s_old = jnp.where(mk, s_old, NEG)
            s_n = jnp.where(mn, s_new, NEG)
            mx = jnp.maximum(jnp.max(s_old, axis=-1, keepdims=True), jnp.max(s_n, axis=-1, keepdims=True))
            p_old = jnp.where(mk, jnp.exp(s_old - mx), 0.0)
            p_n = jnp.where(mn, jnp.exp(s_n - mx), 0.0)
            den = jnp.sum(p_old, axis=-1, keepdims=True) + jnp.sum(p_n, axis=-1, keepdims=True)
            o = (jnp.dot(p_old.astype(BF16), vq, preferred_element_type=F32)
                 + jnp.dot(p_n.astype(BF16), vnew, preferred_element_type=F32)) / den
            o_s = jnp.where(qrow == qi, o, o_s)

        kwo = kwo_ref[b * l_win:(b + 1) * l_win, :].astype(BF16)
        vwo = vwo_ref[b * l_win:(b + 1) * l_win, :].astype(BF16)
        kwn = kwn_ref[b * dec_seq:(b + 1) * dec_seq, :].astype(BF16)
        vwn = vwn_ref[b * dec_seq:(b + 1) * dec_seq, :].astype(BF16)
        qpos = past_len + qrow
        wpos_o = past_len - l_win + lax.broadcasted_iota(jnp.int32, (1, l_win), 1)
        wpos_n = past_len + newj
        d_o = qpos - wpos_o
        d_n = qpos - wpos_n
        m_o = (d_o >= 0) & (d_o <= WINDOW) & (wpos_o >= 0)
        m_n = (d_n >= 0) & (d_n <= WINDOW)
        sw_o = jnp.where(m_o, lax.dot_general(q, kwo, nt, preferred_element_type=F32) * scale, NEG)
        sw_n = jnp.where(m_n, lax.dot_general(q, kwn, nt, preferred_element_type=F32) * scale, NEG)
        mx = jnp.maximum(jnp.max(sw_o, axis=-1, keepdims=True), jnp.max(sw_n, axis=-1, keepdims=True))
        pw_o = jnp.where(m_o, jnp.exp(sw_o - mx), 0.0)
        pw_n = jnp.where(m_n, jnp.exp(sw_n - mx), 0.0)
        den = jnp.sum(pw_o, axis=-1, keepdims=True) + jnp.sum(pw_n, axis=-1, keepdims=True)
        o_w = (jnp.dot(pw_o.astype(BF16), vwo, preferred_element_type=F32)
               + jnp.dot(pw_n.astype(BF16), vwn, preferred_element_type=F32)) / den

        gb = gt[b * dec_seq:(b + 1) * dec_seq, :]
        for h in range(hpg):
            sl = slice(h * dec_seq, (h + 1) * dec_seq)
            g0 = gb[:, 0 * hpg + h:0 * hpg + h + 1]
            g1 = gb[:, 1 * hpg + h:1 * hpg + h + 1]
            g2 = gb[:, 2 * hpg + h:2 * hpg + h + 1]
            o_c = oc_ref[b * dec_seq:(b + 1) * dec_seq, h * HEAD_DIM:(h + 1) * HEAD_DIM]
            of32[b * dec_seq:(b + 1) * dec_seq, h * HEAD_DIM:(h + 1) * HEAD_DIM] = (
                g0 * o_c + g1 * o_s[sl, :] + g2 * o_w[sl, :])
    o_ref[...] = of32[...].astype(o_ref.dtype)


def _nsa_sample(q, gates, kc_s, kv_f32, cache2d, win2d, pt_flat, o_prompt, *, m_prompt, dec_batch,
                dec_seq, hpg, past_len, n_pages, page_size, l_win):
    ms = dec_batch * dec_seq
    gw = hpg * HEAD_DIM
    ncp = past_len // STRIDE
    t_all = past_len + dec_seq
    ns = -(-t_all // L_SLC)
    ns_lanes = -(-ns // LANES) * LANES
    rb = m_prompt // ms
    sel = functools.partial(_nsa_sample_select_kernel, dec_batch=dec_batch, dec_seq=dec_seq, hpg=hpg,
                            past_len=past_len, ncp=ncp, ns=ns, ns_lanes=ns_lanes)
    oc, idx = pl.pallas_call(
        sel,
        out_shape=[jax.ShapeDtypeStruct((ms, N_KV * gw), F32),
                   jax.ShapeDtypeStruct((N_KV * ms, LANES), jnp.int32)],
        grid=(N_KV,),
        in_specs=[pl.BlockSpec((ms, gw), lambda g: (rb, g)),
                  pl.BlockSpec((dec_batch * ncp, HEAD_DIM), lambda g: (0, g)),
                  pl.BlockSpec((dec_batch * ncp, HEAD_DIM), lambda g: (0, N_KV + g))],
        out_specs=[pl.BlockSpec((ms, gw), lambda g: (0, g)),
                   pl.BlockSpec((ms, LANES), lambda g: (g, 0))],
        compiler_params=_params(("parallel",)),
        name="nsa_sample_select",
    )(q, kc_s, kc_s)
    idx_flat = idx[:, :N_SEL].reshape(-1)

    att = functools.partial(_nsa_sample_attend_kernel, dec_batch=dec_batch, dec_seq=dec_seq, hpg=hpg,
                            past_len=past_len, n_pages=n_pages, page_size=page_size, l_win=l_win)

    def sm(f):
        return lambda g, idx_r, pt_r: f(g)

    return pl.pallas_call(
        att,
        out_shape=jax.ShapeDtypeStruct(o_prompt.shape, o_prompt.dtype),
        grid_spec=pltpu.PrefetchScalarGridSpec(
            num_scalar_prefetch=2,
            grid=(N_KV,),
            in_specs=[pl.BlockSpec((ms, gw), sm(lambda g: (rb, g))),
                      pl.BlockSpec((ms, LANES), sm(lambda g: (rb, g))),
                      pl.BlockSpec((ms, gw), sm(lambda g: (0, g))),
                      pl.BlockSpec((ms, HEAD_DIM), sm(lambda g: (rb, 2 * N_KV + g))),
                      pl.BlockSpec((ms, HEAD_DIM), sm(lambda g: (rb, 3 * N_KV + g))),
                      pl.BlockSpec((dec_batch * l_win, HEAD_DIM), sm(lambda g: (0, g))),
                      pl.BlockSpec((dec_batch * l_win, HEAD_DIM), sm(lambda g: (0, N_KV + g))),
                      pl.BlockSpec((ms, HEAD_DIM), sm(lambda g: (rb, 4 * N_KV + g))),
                      pl.BlockSpec((ms, HEAD_DIM), sm(lambda g: (rb, 5 * N_KV + g))),
                      pl.BlockSpec(memory_space=pl.ANY),
                      pl.BlockSpec(memory_space=pl.ANY)],
            out_specs=pl.BlockSpec((ms, gw), sm(lambda g: (rb, g))),
            scratch_shapes=[pltpu.VMEM((dec_seq, N_SEL * L_SLC, HEAD_DIM), F32),
                            pltpu.VMEM((dec_seq, N_SEL * L_SLC, HEAD_DIM), F32),
                            pltpu.VMEM((ms, gw), F32),
                            pltpu.SemaphoreType.DMA((2,))]),
        input_output_aliases={12: 0},
        compiler_params=_params(("arbitrary",)),
        name="nsa_sample_attend",
    )(idx_flat, pt_flat, q, gates, oc, kv_f32, kv_f32, win2d, win2d, kv_f32, kv_f32, cache2d, o_prompt)


def _router_kernel(x_ref, g_ref, r_ref, hp_ref, idx_ref, wts_ref, *, n_experts):
    x = x_ref[...]
    h = x * lax.rsqrt(jnp.mean(x * x, axis=-1, keepdims=True) + EPS) * g_ref[...]
    half = h.shape[1] // 2
    bits = pltpu.bitcast(h.astype(BF16).astype(F32), jnp.uint32)
    hp_ref[...] = (bits[:, :half] >> 16) | (bits[:, half:] & jnp.uint32(0xFFFF0000))
    r_hi, r_mid, r_lo = _split3(r_ref[...])
    h_hi, h_mid, h_lo = _split3(h)

    def d(a, b):
        return jnp.dot(a, b, preferred_element_type=F32)

    logits = (d(h_lo, r_hi) + d(h_hi, r_lo) + d(h_mid, r_mid)) + (d(h_mid, r_hi) + d(h_hi, r_mid)) + d(h_hi, r_hi)
    lane = lax.broadcasted_iota(jnp.int32, logits.shape, 1)
    logits = jnp.where(lane < n_experts, logits, -jnp.inf)
    v1 = jnp.max(logits, axis=-1, keepdims=True)
    i1 = jnp.min(jnp.where(logits == v1, lane, LANES), axis=-1, keepdims=True)
    rest = jnp.where(lane == i1, -jnp.inf, logits)
    v2 = jnp.max(rest, axis=-1, keepdims=True)
    i2 = jnp.min(jnp.where(rest == v2, lane, LANES), axis=-1, keepdims=True)
    e2 = jnp.exp(v2 - v1)
    w1 = 1.0 / (1.0 + e2)
    w2 = e2 / (1.0 + e2)
    idx_ref[...] = jnp.where(lane == 0, i1, jnp.where(lane == 1, i2, 0))
    wts_ref[...] = jnp.where(lane == 0, w1, jnp.where(lane == 1, w2, 0.0))


def _router(x, gain, router_p, n_experts):
    m, d = x.shape
    tm = _pick(m, (192, 128, 64, 32, 16))
    return pl.pallas_call(
        functools.partial(_router_kernel, n_experts=n_experts),
        out_shape=[jax.ShapeDtypeStruct((m, d // 2), jnp.uint32),
                   jax.ShapeDtypeStruct((m, LANES), jnp.int32),
                   jax.ShapeDtypeStruct((m, LANES), F32)],
        grid=(m // tm,),
        in_specs=[pl.BlockSpec((tm, d), lambda i: (i, 0)),
                  pl.BlockSpec((1, d), lambda i: (0, 0)),
                  pl.BlockSpec((d, LANES), lambda i: (0, 0))],
        out_specs=[pl.BlockSpec((tm, d // 2), lambda i: (i, 0)),
                   pl.BlockSpec((tm, LANES), lambda i: (i, 0)),
                   pl.BlockSpec((tm, LANES), lambda i: (i, 0))],
        compiler_params=_params(("parallel",)),
        name="moe_router",
    )(x, gain, router_p)


def _moe_plan(idx2, n_experts, rc, nch, sub):
    e = idx2.reshape(-1)
    onehot = (e[:, None] == jnp.arange(n_experts, dtype=jnp.int32)[None, :]).astype(jnp.int32)
    csum = jnp.cumsum(onehot, axis=0)
    rank = jnp.take_along_axis(csum, e[:, None], axis=1)[:, 0] - 1
    counts = csum[-1]
    nchunks = (counts + rc - 1) // rc
    cend = jnp.cumsum(nchunks)
    cstart = cend - nchunks
    pos = (cstart[e] + rank // rc) * rc + rank % rc
    n_active = cend[-1]
    c = jnp.arange(nch, dtype=jnp.int32)
    cx = jnp.minimum(c, n_active - 1)
    ce = jnp.sum((cend[None, :] <= cx[:, None]).astype(jnp.int32), axis=1)
    rows = jnp.clip(counts[ce] - (cx - cstart[ce]) * rc, 0, rc)
    nsub = jnp.where(c < n_active, (rows + sub - 1) // sub, 0)
    return pos.astype(jnp.int32), nsub.astype(jnp.int32), cx.astype(jnp.int32), ce


def _moe_scatter_kernel(pos_ref, hp_ref, xs_in_ref, xs_ref, sems, *, tt):
    del xs_in_ref

    def copies(r):
        return [pltpu.make_async_copy(hp_ref.at[pl.ds(r, 1), :],
                                      xs_ref.at[pl.ds(pos_ref[0, 0, TOP_K * r + s], 1), :], sems.at[s])
                for s in range(TOP_K)]

    def start(r, carry):
        for cp in copies(r):
            cp.start()
        return carry

    def wait(r, carry):
        for cp in copies(r):
            cp.wait()
        return carry

    lax.fori_loop(0, tt, start, 0)
    lax.fori_loop(0, tt, wait, 0)


def _moe_scatter(hp, pos, n_rows):
    m, dp = hp.shape
    tt = _pick(m, (192, 128, 64, 32, 16))
    pos3 = pos.reshape(m // tt, 1, TOP_K * tt)
    return pl.pallas_call(
        functools.partial(_moe_scatter_kernel, tt=tt),
        out_shape=jax.ShapeDtypeStruct((n_rows, dp), hp.dtype),
        grid=(m // tt,),
        in_specs=[pl.BlockSpec((1, 1, TOP_K * tt), lambda i: (i, 0, 0), memory_space=pltpu.SMEM),
                  pl.BlockSpec((tt, dp), lambda i: (i, 0)),
                  pl.BlockSpec(memory_space=pl.ANY)],
        out_specs=pl.BlockSpec(memory_space=pl.ANY),
        scratch_shapes=[pltpu.SemaphoreType.DMA((TOP_K,))],
        input_output_aliases={2: 0},
        compiler_params=_params(("arbitrary",)),
        name="moe_scatter",
    )(pos3, hp, jnp.zeros((n_rows, dp), hp.dtype))


def _unpack_bf16_pair(xp):
    lo = pltpu.bitcast(xp << 16, F32).astype(BF16)
    hi = pltpu.bitcast(xp & jnp.uint32(0xFFFF0000), F32).astype(BF16)
    return lo, hi


def _over_valid_rows(ns, rc, sub, body):
    big = 8 * sub
    for b in range(rc // big):
        @pl.when(ns >= 8 * (b + 1))
        def _(b=b):
            body(pl.ds(b * big, big))

    off = (ns // 8) * big
    rem = ns % 8
    for bit in (4, 2, 1):
        @pl.when((rem & bit) != 0)
        def _(bit=bit, off=off):
            body(pl.ds(pl.multiple_of(off, sub), bit * sub))

        off = off + (rem & bit) * sub


def _moe_gu_kernel(nsub_ref, cx_ref, ce_ref, x_ref, wgl_ref, wgh_ref, wul_ref, wuh_ref, o_ref,
                   accg, accu, wb, *, nk, sub):
    del cx_ref, ce_ref
    c = pl.program_id(0)
    k = pl.program_id(2)
    ns = nsub_ref[c]

    @pl.when(ns > 0)
    def _():
        @pl.when(k == 0)
        def _():
            accg[...] = jnp.zeros_like(accg)
            accu[...] = jnp.zeros_like(accu)

        for n, w_ref in enumerate((wgl_ref, wgh_ref, wul_ref, wuh_ref)):
            wb[n] = w_ref[...].astype(BF16)

        def body(rows):
            lo, hi = _unpack_bf16_pair(x_ref[rows, :])
            accg[rows, :] += (jnp.dot(lo, wb[0], preferred_element_type=F32)
                              + jnp.dot(hi, wb[1], preferred_element_type=F32))
            accu[rows, :] += (jnp.dot(lo, wb[2], preferred_element_type=F32)
                              + jnp.dot(hi, wb[3], preferred_element_type=F32))

        _over_valid_rows(ns, accg.shape[0], sub, body)

        @pl.when(k == nk - 1)
        def _():
            g = accg[...]
            o_ref[...] = (g * _sigmoid(g) * accu[...]).astype(o_ref.dtype)


def _moe_down_kernel(nsub_ref, cx_ref, ce_ref, x_ref, w_ref, o_ref, wb, *, sub):
    del cx_ref, ce_ref
    c = pl.program_id(0)
    k = pl.program_id(2)
    ns = nsub_ref[c]

    @pl.when(ns > 0)
    def _():
        @pl.when(k == 0)
        def _():
            o_ref[...] = jnp.zeros_like(o_ref)

        wb[...] = w_ref[...].astype(BF16)

        def body(rows):
            o_ref[rows, :] += jnp.dot(x_ref[rows, :], wb[...], preferred_element_type=F32)

        _over_valid_rows(ns, o_ref.shape[0], sub, body)


def _moe_experts(xs, wgu, wdn, nsub, cx, ce, *, d, d_ff, rc, nch, sub):
    half = d // 2
    tkp = _pick(half, (512, 256, 128))
    nk = half // tkp
    tn = _pick(d_ff, (512, 256, 128))
    nj = d_ff // tn

    def frozen(c, a, last, nsub_r):
        return jnp.where(nsub_r[c] > 0, a, last)

    def x_map(c, j, k, nsub_r, cx_r, ce_r):
        return (cx_r[c], frozen(c, k, nk - 1, nsub_r))

    def w_map(col_off, row_off):
        def f(c, j, k, nsub_r, cx_r, ce_r):
            return (ce_r[c] * (d // tkp) + row_off // tkp + frozen(c, k, nk - 1, nsub_r),
                    col_off // tn + frozen(c, j, nj - 1, nsub_r))
        return f

    def o_map(c, j, k, nsub_r, cx_r, ce_r):
        return (cx_r[c], frozen(c, j, nj - 1, nsub_r))

    act = pl.pallas_call(
        functools.partial(_moe_gu_kernel, nk=nk, sub=sub),
        out_shape=jax.ShapeDtypeStruct((nch * rc, d_ff), BF16),
        grid_spec=pltpu.PrefetchScalarGridSpec(
            num_scalar_prefetch=3,
            grid=(nch, nj, nk),
            in_specs=[pl.BlockSpec((rc, tkp), x_map),
                      pl.BlockSpec((tkp, tn), w_map(0, 0)),
                      pl.BlockSpec((tkp, tn), w_map(0, half)),
                      pl.BlockSpec((tkp, tn), w_map(d_ff, 0)),
                      pl.BlockSpec((tkp, tn), w_map(d_ff, half))],
            out_specs=pl.BlockSpec((rc, tn), o_map),
            scratch_shapes=[pltpu.VMEM((rc, tn), F32), pltpu.VMEM((rc, tn), F32),
                            pltpu.VMEM((4, tkp, tn), BF16)]),
        compiler_params=_params(("arbitrary", "arbitrary", "arbitrary")),
        name="moe_gu",
    )(nsub, cx, ce, xs, wgu, wgu, wgu, wgu)

    tk2 = _pick(d_ff, (1024, 512, 256, 128))
    nk2 = d_ff // tk2
    tn2 = _pick(d, (1024, 512, 256, 128))
    nj2 = d // tn2

    def x2_map(c, j, k, nsub_r, cx_r, ce_r):
        return (cx_r[c], frozen(c, k, nk2 - 1, nsub_r))

    def w2_map(c, j, k, nsub_r, cx_r, ce_r):
        return (ce_r[c] * nk2 + frozen(c, k, nk2 - 1, nsub_r), frozen(c, j, nj2 - 1, nsub_r))

    def o2_map(c, j, k, nsub_r, cx_r, ce_r):
        return (cx_r[c], frozen(c, j, nj2 - 1, nsub_r))

    return pl.pallas_call(
        functools.partial(_moe_down_kernel, sub=sub),
        out_shape=jax.ShapeDtypeStruct((nch * rc, d), F32),
        grid_spec=pltpu.PrefetchScalarGridSpec(
            num_scalar_prefetch=3,
            grid=(nch, nj2, nk2),
            in_specs=[pl.BlockSpec((rc, tk2), x2_map),
                      pl.BlockSpec((tk2, tn2), w2_map)],
            out_specs=pl.BlockSpec((rc, tn2), o2_map),
            scratch_shapes=[pltpu.VMEM((tk2, tn2), BF16)]),
        compiler_params=_params(("arbitrary", "arbitrary", "arbitrary")),
        name="moe_down",
    )(nsub, cx, ce, act, wdn)


def _moe_combine_kernel(pos_ref, x_ref, wts_ref, y_ref, o_ref, ybuf, sems, *, tt):
    i = pl.program_id(0)
    del i

    def copies(r):
        return [pltpu.make_async_copy(y_ref.at[pl.ds(pos_ref[0, 0, TOP_K * r + s], 1), :],
                                      ybuf.at[s, pl.ds(r, 1), :], sems.at[s])
                for s in range(TOP_K)]

    def start(r, carry):
        for cp in copies(r):
            cp.start()
        return carry

    def wait(r, carry):
        for cp in copies(r):
            cp.wait()
        return carry

    lax.fori_loop(0, tt, start, 0)
    lax.fori_loop(0, tt, wait, 0)
    w = wts_ref[...]
    o_ref[...] = x_ref[...] + (w[:, 0:1] * ybuf[0] + w[:, 1:2] * ybuf[1])


def _moe_combine(x, wts, y, pos):
    m, d = x.shape
    tt = _pick(m, (192, 128, 64, 32, 16))
    pos3 = pos.reshape(m // tt, 1, TOP_K * tt)
    return pl.pallas_call(
        functools.partial(_moe_combine_kernel, tt=tt),
        out_shape=jax.ShapeDtypeStruct((m, d), F32),
        grid=(m // tt,),
        in_specs=[pl.BlockSpec((1, 1, TOP_K * tt), lambda i: (i, 0, 0), memory_space=pltpu.SMEM),
                  pl.BlockSpec((tt, d), lambda i: (i, 0)),
                  pl.BlockSpec((tt, LANES), lambda i: (i, 0)),
                  pl.BlockSpec(memory_space=pl.ANY)],
        out_specs=pl.BlockSpec((tt, d), lambda i: (i, 0)),
        scratch_shapes=[pltpu.VMEM((TOP_K, tt, d), F32), pltpu.SemaphoreType.DMA((TOP_K,))],
        compiler_params=_params(("arbitrary",)),
        name="moe_combine",
    )(pos3, x, wts, y)


def kernel(x_prompt, x_sample, state_conv, cache_kv, cache_win, page_table, norm_mix, norm_ffn, conv_w_in, conv_w, conv_w_out, ffn_w_gu, ffn_w_down, moe_router, moe_w_gu, moe_w_down, kv_norm, w_kv, k_norm, cmp_w1, cmp_w2, cmp_pe, w_qg, q_norm, w_o):
    batch, seq, d = x_prompt.shape
    dec_batch, dec_seq, _ = x_sample.shape
    n_pool, page_size = cache_kv.shape[:2]
    n_pages = page_table.shape[1]
    past_len = n_pages * page_size
    l_win = cache_win.shape[1]
    d_ff = ffn_w_down.shape[1]
    n_experts = moe_router.shape[2]
    n_heads = d // HEAD_DIM
    hpg = n_heads // N_KV
    sec_w = N_KV * HEAD_DIM
    m_prompt = batch * seq
    ms = dec_batch * dec_seq
    m = m_prompt + ms
    assert seq & (seq - 1) == 0 and dec_seq & (dec_seq - 1) == 0 and dec_seq >= CONV_W - 1
    assert m_prompt % ms == 0 and ms % 16 == 0 and seq % 128 == 0 and seq >= WINDOW + 128
    assert past_len % L_SLC == 0 and dec_seq <= L_SLC and page_size % L_SLC == 0
    assert norm_mix.shape[0] == 2 and l_win == WINDOW and past_len >= l_win

    x0 = jnp.concatenate([x_prompt.reshape(m_prompt, d), x_sample.reshape(ms, d)], axis=0)
    tm = _row_tile(m)

    (h0,) = _rmsnorm(x0, norm_mix[0:1])
    tn = _pick(d, (256, 128))
    tk = _pick(d, (1024, 512, 256, 128))
    b_gate, u = _matmul(
        h0, conv_w_in[0], col_offsets=(0, d, 2 * d), n_cols=d, tm=tm, tn=tn, tk=tk,
        epilogue=_epi_conv_in, out_shapes=[jax.ShapeDtypeStruct((m, d), F32)] * 2,
        out_specs=[_ij_spec(tm, tn)] * 2, name="conv_in")
    st = state_conv[0]
    zrow = jnp.zeros((dec_batch, dec_seq - 1, d), F32)
    s1 = jnp.concatenate([st[:, 1:2], zrow], axis=1).reshape(ms, d)
    s2 = jnp.concatenate([st[:, 0:1], st[:, 1:2], zrow[:, 1:]], axis=1).reshape(ms, d)
    z = _conv_gate(u, b_gate, conv_w[0], s1, s2, m_prompt=m_prompt, seq=seq, dec_seq=dec_seq)
    tn = _pick(d, (1024, 512, 256, 128))
    tk_r = _pick(d, (512, 256, 128))
    (x1,) = _matmul(
        z, conv_w_out[0], col_offsets=(0,), n_cols=d, tm=tm, tn=tn, tk=tk_r, epilogue=_epi_residual,
        extras=(x0,), extra_specs=(_ij_spec(tm, tn),), out_shapes=[jax.ShapeDtypeStruct((m, d), F32)],
        out_specs=[_ij_spec(tm, tn)], name="conv_out")
    u_p = u[:m_prompt].reshape(batch, seq, d)
    conv_prompt = u_p[:, seq - (CONV_W - 1):][None]
    conv_sample = u[m_prompt:].reshape(dec_batch, dec_seq, d)[:, dec_seq - (CONV_W - 1):][None]

    (h1,) = _rmsnorm(x1, norm_ffn[0:1])
    tnf = _pick(d_ff, (512, 256, 128))
    (act,) = _matmul(
        h1, ffn_w_gu[0], col_offsets=(0, d_ff), n_cols=d_ff, tm=tm, tn=tnf, tk=tk, epilogue=_epi_swiglu,
        out_shapes=[jax.ShapeDtypeStruct((m, d_ff), BF16)], out_specs=[_ij_spec(tm, tnf)], name="ffn_gu")
    tkf = _pick(d_ff, (512, 256, 128))
    (x2,) = _matmul(
        act, ffn_w_down[0], col_offsets=(0,), n_cols=d, tm=tm, tn=tn, tk=tkf, epilogue=_epi_residual,
        extras=(x1,), extra_specs=(_ij_spec(tm, tn),), out_shapes=[jax.ShapeDtypeStruct((m, d), F32)],
        out_specs=[_ij_spec(tm, tn)], name="ffn_down")

    hkv, h2 = _rmsnorm(x2, jnp.stack([kv_norm, norm_mix[1]]))
    ones = jnp.ones((HEAD_DIM,), F32)
    kv_gain = jnp.stack([ones, ones, k_norm[1], ones, k_norm[2], ones]).reshape(2 * N_BRANCH, 1, HEAD_DIM)
    kv_f32, kv_bf = _matmul(
        hkv, w_kv, col_offsets=(0,), n_cols=2 * N_BRANCH * sec_w, tm=tm, tn=sec_w, tk=tk, epilogue=_epi_kv,
        extras=(kv_gain,), extra_specs=(pl.BlockSpec((None, 1, HEAD_DIM), lambda i, j, k: (j, 0, 0)),),
        out_shapes=[jax.ShapeDtypeStruct((m, 2 * N_BRANCH * sec_w), F32),
                    jax.ShapeDtypeStruct((m, 2 * N_BRANCH * sec_w), BF16)],
        out_specs=[_ij_spec(tm, sec_w)] * 2, name="kv_proj")
    kv_p = kv_f32[:m_prompt].reshape(batch, seq, 2 * N_BRANCH, N_KV, HEAD_DIM)
    kv_s = kv_f32[m_prompt:].reshape(dec_batch, dec_seq, 2 * N_BRANCH, N_KV, HEAD_DIM)
    kv_prompt = kv_p[:, :, :4]
    win_prompt = kv_p[:, seq - min(WINDOW, seq):, 4:]
    kv_sample = kv_s[:, :, :4]
    win_sample = jnp.concatenate([cache_win, kv_s[:, :, 4:]], axis=1)[:, dec_seq:]

    wq = w_qg[0]
    (q,) = _matmul(
        h2, wq, col_offsets=(0,), n_cols=d, tm=tm, tn=tn, tk=tk_r, epilogue=_epi_q,
        extras=(q_norm[0:1],), extra_specs=(pl.BlockSpec((1, HEAD_DIM), lambda i, j, k: (0, 0)),),
        out_shapes=[jax.ShapeDtypeStruct((m, d), BF16)], out_specs=[_ij_spec(tm, tn)], name="q_proj")
    wg = wq[:, d:].reshape(d, N_KV, hpg, N_BRANCH).transpose(0, 1, 3, 2).reshape(d, N_KV, N_BRANCH * hpg)
    wg = jnp.pad(wg, ((0, 0), (0, 0), (0, LANES - N_BRANCH * hpg))).reshape(d, N_KV * LANES)
    (gates,) = _matmul(
        h2, wg, col_offsets=(0,), n_cols=N_KV * LANES, tm=tm, tn=N_KV * LANES, tk=tk, epilogue=_epi_gate,
        out_shapes=[jax.ShapeDtypeStruct((m, N_KV * LANES), F32)], out_specs=[_ij_spec(tm, N_KV * LANES)],
        name="gate_proj")

    w1r = cmp_w1.reshape(2, R_CMP, STRIDE * HEAD_DIM, HEAD_DIM)
    per = cmp_pe.reshape(2, R_CMP, 1, STRIDE * HEAD_DIM)
    kn0 = k_norm[0:1]
    kc_p = _compress_prompt(kv_f32, w1r, per, cmp_w2, kn0, batch=batch, seq=seq)
    pt_flat = page_table.reshape(-1)
    kc_s = _compress_sample(cache_kv, pt_flat, w1r, per, cmp_w2, kn0, dec_batch=dec_batch, n_pages=n_pages,
                            page_size=page_size)

    o = _nsa_prompt(q, gates, kc_p, kv_bf, m_total=m, batch=batch, seq=seq, hpg=hpg)
    win2d = cache_win.reshape(dec_batch * l_win, 2 * sec_w)
    o = _nsa_sample(q, gates, kc_s, kv_f32, cache_kv, win2d, pt_flat, o, m_prompt=m_prompt,
                    dec_batch=dec_batch, dec_seq=dec_seq, hpg=hpg, past_len=past_len, n_pages=n_pages,
                    page_size=page_size, l_win=l_win)
    (x3,) = _matmul(
        o, w_o[0], col_offsets=(0,), n_cols=d, tm=tm, tn=tn, tk=tk_r, epilogue=_epi_residual,
        extras=(x2,), extra_specs=(_ij_spec(tm, tn),), out_shapes=[jax.ShapeDtypeStruct((m, d), F32)],
        out_specs=[_ij_spec(tm, tn)], name="attn_out")

    router_p = jnp.pad(moe_router[0], ((0, 0), (0, LANES - n_experts)))
    hp, ridx, wts = _router(x3, norm_ffn[1:2], router_p, n_experts)
    wgu = moe_w_gu[0].reshape(n_experts * d, 2 * d_ff)
    wdn = moe_w_down[0].reshape(n_experts * d_ff, d)
    sub = MOE_SUB_ROWS
    rc = -(-(TOP_K * m * 12 // (10 * n_experts)) // (2 * sub)) * (2 * sub)
    nch = TOP_K * m // rc + n_experts
    pos, nsub, cx, ce = _moe_plan(ridx[:, :TOP_K], n_experts, rc, nch, sub)
    xs = _moe_scatter(hp, pos, nch * rc)
    y = _moe_experts(xs, wgu, wdn, nsub, cx, ce, d=d, d_ff=d_ff, rc=rc, nch=nch, sub=sub)
    x4 = _moe_combine(x3, wts, y, pos)

    y_prompt = x4[:m_prompt].reshape(batch, seq, d)
    y_sample = x4[m_prompt:].reshape(dec_batch, dec_seq, d)
    return (y_prompt, y_sample, conv_prompt, kv_prompt, win_prompt, conv_sample, kv_sample, win_sample)
```

```python
import functools

import jax
import jax.numpy as jnp
from jax import lax
from jax.experimental import pallas as pl
from jax.experimental.pallas import tpu as pltpu

HEAD_DIM = 128
N_KV = 4
N_BRANCH = 3
L_CMP = 32
STRIDE = 16
R_CMP = L_CMP // STRIDE
L_SLC = 64
N_SEL = 16
WINDOW = 512
TOP_K = 2
CONV_W = 3
EPS = 1e-6
NEG = -1e30
FORCE = 1e6

LANES = 128
MOE_SUB_ROWS = 128
FLASH_SLAB_ROWS = 64
M_FLOOR = -1e29
LOG2_E = 1.4426950408889634
VMEM_LIMIT = 56 * 1024 * 1024

F32 = jnp.float32
BF16 = jnp.bfloat16


def _pick(n, candidates):
    for c in candidates:
        if c <= n and n % c == 0:
            return c
    return n


def _params(sem):
    return pltpu.CompilerParams(dimension_semantics=sem, vmem_limit_bytes=VMEM_LIMIT)


def _sigmoid(x):
    return 0.5 * jnp.tanh(0.5 * x) + 0.5


def _div_pow2(x, n):
    assert n & (n - 1) == 0
    return jnp.right_shift(x, n.bit_length() - 1)


def _rmsnorm_kernel(x_ref, g_ref, *o_refs):
    x = x_ref[...]
    y = x * lax.rsqrt(jnp.mean(x * x, axis=-1, keepdims=True) + EPS)
    for n, o_ref in enumerate(o_refs):
        o_ref[...] = (y * g_ref[n:n + 1, :]).astype(o_ref.dtype)


def _rmsnorm(x, gains):
    m, d = x.shape
    n = gains.shape[0]
    tm = _pick(m, (192, 128, 64, 32, 16))
    outs = pl.pallas_call(
        _rmsnorm_kernel,
        out_shape=[jax.ShapeDtypeStruct((m, d), BF16)] * n,
        grid=(m // tm,),
        in_specs=[pl.BlockSpec((tm, d), lambda i: (i, 0)),
                  pl.BlockSpec((n, d), lambda i: (0, 0))],
        out_specs=[pl.BlockSpec((tm, d), lambda i: (i, 0))] * n,
        compiler_params=_params(("parallel",)),
        name="rmsnorm",
    )(x, gains)
    return outs


def _mm_kernel(*refs, nk, n_w, n_extra, n_out, epilogue):
    x_ref = refs[0]
    w_refs = refs[1:1 + n_w]
    extra = refs[1 + n_w:1 + n_w + n_extra]
    out_refs = refs[1 + n_w + n_extra:1 + n_w + n_extra + n_out]
    acc_refs = refs[1 + n_w + n_extra + n_out:]
    i = pl.program_id(0)
    j = pl.program_id(1)
    k = pl.program_id(2)

    def step(first, last):
        x = x_ref[...]
        vals = []
        for w_ref, a in zip(w_refs, acc_refs):
            part = jnp.dot(x, w_ref[...].astype(BF16), preferred_element_type=F32)
            vals.append(part if first else a[...] + part)
        if last:
            epilogue((i, j), vals, extra, out_refs)
        else:
            for a, v in zip(acc_refs, vals):
                a[...] = v

    if nk == 1:
        step(True, True)
    else:
        pl.when(k == 0)(lambda: step(True, False))
        if nk > 2:
            pl.when((k > 0) & (k < nk - 1))(lambda: step(False, False))
        pl.when(k == nk - 1)(lambda: step(False, True))


def _matmul(x, w, *, col_offsets, n_cols, tm, tn, tk, epilogue, extras=(), extra_specs=(),
            out_shapes, out_specs, w_row_offset=0, name):
    m, kdim = x.shape
    nk = kdim // tk
    n_w = len(col_offsets)
    in_specs = [pl.BlockSpec((tm, tk), lambda i, j, k: (i, k))]
    for off in col_offsets:
        in_specs.append(pl.BlockSpec(
            (tk, tn), lambda i, j, k, off=off: (k + w_row_offset // tk, j + off // tn)))
    in_specs += list(extra_specs)
    kern = functools.partial(_mm_kernel, nk=nk, n_w=n_w, n_extra=len(extras),
                             n_out=len(out_shapes), epilogue=epilogue)
    return pl.pallas_call(
        kern,
        out_shape=out_shapes,
        grid=(m // tm, n_cols // tn, nk),
        in_specs=in_specs,
        out_specs=out_specs,
        scratch_shapes=[pltpu.VMEM((tm, tn), F32)] * n_w,
        compiler_params=_params(("parallel", "parallel", "arbitrary")),
        name=name,
    )(x, *([w] * n_w), *extras)


def _row_tile(m):
    return _pick(m, (2064, 2048, 1024, 688, 512, 256, 192, 128, 64, 32, 16))


def _ij_spec(tm, tn):
    return pl.BlockSpec((tm, tn), lambda i, j, k: (i, j))


def _epi_conv_in(ids, accs, extra, outs):
    b, c, v = accs
    outs[0][...] = b
    outs[1][...] = c * v


def _epi_residual(ids, accs, extra, outs):
    outs[0][...] = extra[0][...] + accs[0]


def _epi_swiglu(ids, accs, extra, outs):
    g, u = accs
    outs[0][...] = (g * _sigmoid(g) * u).astype(outs[0].dtype)


def _head_rmsnorm(a, gain):
    parts = []
    for h in range(a.shape[1] // HEAD_DIM):
        ah = a[:, h * HEAD_DIM:(h + 1) * HEAD_DIM]
        ms = jnp.mean(ah * ah, axis=-1, keepdims=True)
        parts.append(ah * lax.rsqrt(ms + EPS) * gain)
    return jnp.concatenate(parts, axis=1)


def _epi_kv(ids, accs, extra, outs):
    _, j = ids
    a = accs[0]
    normed = _head_rmsnorm(a, extra[0][...])
    y = jnp.where((j == 2) | (j == 4), normed, a)
    outs[0][...] = y
    outs[1][...] = y.astype(BF16)


def _epi_q(ids, accs, extra, outs):
    outs[0][...] = _head_rmsnorm(accs[0], extra[0][...]).astype(BF16)


def _epi_gate(ids, accs, extra, outs):
    outs[0][...] = _sigmoid(accs[0])


def _conv_kernel(u_ref, up_ref, b_ref, cw_ref, s1_ref, s2_ref, z_ref, *, tm, m_prompt, seq, dec_seq,
                 n_row_tiles):
    i = pl.program_id(0)
    u = u_ref[...]
    prev = up_ref[...]
    loc = lax.broadcasted_iota(jnp.int32, (tm, 1), 0)
    r = i * tm + loc
    u1 = pltpu.roll(u, 1, 0)
    u1 = jnp.where(loc == 0, prev[7:8, :], u1)
    u2 = pltpu.roll(u, 2, 0)
    u2 = jnp.where(loc == 0, prev[6:7, :], jnp.where(loc == 1, prev[7:8, :], u2))
    t = jnp.where(r < m_prompt, r & (seq - 1), (r - m_prompt) & (dec_seq - 1))
    u1 = jnp.where(t >= 1, u1, 0.0)
    u2 = jnp.where(t >= 2, u2, 0.0)
    w0 = cw_ref[0:1, :]
    w1 = cw_ref[1:2, :]
    w2 = cw_ref[2:3, :]
    conv = w2 * u + w1 * u1 + w0 * u2
    z_ref[...] = (b_ref[...] * conv).astype(z_ref.dtype)
    ms = s1_ref.shape[0]

    @pl.when(i == n_row_tiles - 1)
    def _():
        tail = conv[tm - ms:, :] + w1 * s1_ref[...] + w0 * s2_ref[...]
        z_ref[tm - ms:, :] = (b_ref[tm - ms:, :] * tail).astype(z_ref.dtype)


def _conv_gate(u, b, cw, s1, s2, *, m_prompt, seq, dec_seq):
    m, d = u.shape
    ms = s1.shape[0]
    tm = _row_tile(m)
    tc = _pick(d, (512, 256, 128))
    n_row_tiles = m // tm
    kern = functools.partial(_conv_kernel, tm=tm, m_prompt=m_prompt, seq=seq, dec_seq=dec_seq,
                             n_row_tiles=n_row_tiles)
    return pl.pallas_call(
        kern,
        out_shape=jax.ShapeDtypeStruct((m, d), BF16),
        grid=(n_row_tiles, d // tc),
        in_specs=[pl.BlockSpec((tm, tc), lambda i, j: (i, j)),
                  pl.BlockSpec((8, tc), lambda i, j: (jnp.maximum(i * (tm // 8) - 1, 0), j)),
                  pl.BlockSpec((tm, tc), lambda i, j: (i, j)),
                  pl.BlockSpec((CONV_W, tc), lambda i, j: (0, j)),
                  pl.BlockSpec((ms, tc), lambda i, j: (0, j)),
                  pl.BlockSpec((ms, tc), lambda i, j: (0, j))],
        out_specs=pl.BlockSpec((tm, tc), lambda i, j: (i, j)),
        compiler_params=_params(("parallel", "parallel")),
        name="conv_gate",
    )(u, u, b, cw, s1, s2)


def _cmp_stage1_compute(head_refs, w1_ref, pe_ref, top_ref, bot_ref, nch):
    for sec in range(2):
        wt = w1_ref[sec, 0].astype(BF16)
        wb = w1_ref[sec, 1].astype(BF16)
        pt = pe_ref[sec, 0]
        pb = pe_ref[sec, 1]
        rows = []
        for g in range(N_KV):
            ref = head_refs[sec * N_KV + g]
            cols = [ref[pl.ds(s, nch, stride=STRIDE), :] for s in range(STRIDE)]
            rows.append(jnp.concatenate(cols, axis=1))
        a = jnp.concatenate(rows, axis=0)
        top = jnp.dot((a + pt).astype(BF16), wt, preferred_element_type=F32)
        bot = jnp.dot((a + pb).astype(BF16), wb, preferred_element_type=F32)
        for g in range(N_KV):
            hh = sec * N_KV + g
            top_ref[:, hh * HEAD_DIM:(hh + 1) * HEAD_DIM] = top[g * nch:(g + 1) * nch, :]
            bot_ref[:, hh * HEAD_DIM:(hh + 1) * HEAD_DIM] = bot[g * nch:(g + 1) * nch, :]


def _cmp1_prompt_kernel(*refs, nch):
    n_heads = 2 * N_KV
    head_refs = refs[:n_heads]
    w1_ref, pe_ref, top_ref, bot_ref = refs[n_heads:]
    _cmp_stage1_compute(head_refs, w1_ref, pe_ref, top_ref, bot_ref, nch)


def _cmp1_sample_kernel(pt_ref, cache_ref, w1_ref, pe_ref, top_ref, bot_ref, buf_ref, sem, *,
                        pages_per_step, n_pages, page_size, n_groups, n_steps):
    step = pl.program_id(0) * n_groups + pl.program_id(1)
    slot = step % 2
    n_heads = 2 * N_KV

    def copies(st, sl):
        first = (st // n_groups) * n_pages + (st % n_groups) * pages_per_step
        out = []
        for p in range(pages_per_step):
            page = pt_ref[first + p]
            for hh in range(n_heads):
                out.append(pltpu.make_async_copy(
                    cache_ref.at[page, :, hh // N_KV, hh % N_KV, :],
                    buf_ref.at[sl, hh, pl.ds(p * page_size, page_size), :],
                    sem.at[sl]))
        return out

    @pl.when(step == 0)
    def _():
        for cp in copies(0, 0):
            cp.start()

    @pl.when(step + 1 < n_steps)
    def _():
        for cp in copies(step + 1, 1 - slot):
            cp.start()

    for cp in copies(step, slot):
        cp.wait()
    _cmp_stage1_compute([buf_ref.at[slot, hh] for hh in range(n_heads)], w1_ref, pe_ref, top_ref, bot_ref,
                        pages_per_step * page_size // STRIDE)


def _cmp2_kernel(top_ref, bot_ref, w2_ref, kn_ref, o_ref, *, nch):
    pre = top_ref[...] + pltpu.roll(bot_ref[...], nch - 1, 0)
    a = (pre * _sigmoid(pre)).astype(BF16)
    for sec in range(2):
        w2 = w2_ref[sec].astype(BF16)
        for g in range(N_KV):
            hh = sec * N_KV + g
            y = jnp.dot(a[:, hh * HEAD_DIM:(hh + 1) * HEAD_DIM], w2, preferred_element_type=F32)
            if sec == 0:
                y = _head_rmsnorm(y, kn_ref[...])
            o_ref[:, hh * HEAD_DIM:(hh + 1) * HEAD_DIM] = y.astype(o_ref.dtype)


def _cmp_stage2(top, bot, w2, kn0, nch):
    rows, width = top.shape
    return pl.pallas_call(
        functools.partial(_cmp2_kernel, nch=nch),
        out_shape=jax.ShapeDtypeStruct((rows, width), BF16),
        grid=(rows // nch,),
        in_specs=[pl.BlockSpec((nch, width), lambda b: (b, 0)),
                  pl.BlockSpec((nch, width), lambda b: (b, 0)),
                  pl.BlockSpec(w2.shape, lambda b: (0, 0, 0)),
                  pl.BlockSpec(kn0.shape, lambda b: (0, 0))],
        out_specs=pl.BlockSpec((nch, width), lambda b: (b, 0)),
        compiler_params=_params(("parallel",)),
        name="cmp_stage2",
    )(top, bot, w2, kn0)


def _compress_prompt(kv_f32, w1r, per, w2, kn0, *, batch, seq):
    width = 2 * N_KV * HEAD_DIM
    nch = seq // STRIDE
    top, bot = pl.pallas_call(
        functools.partial(_cmp1_prompt_kernel, nch=nch),
        out_shape=[jax.ShapeDtypeStruct((batch * nch, width), F32)] * 2,
        grid=(batch,),
        in_specs=[pl.BlockSpec((seq, HEAD_DIM), lambda b, hh=hh: (b, hh)) for hh in range(2 * N_KV)]
        + [pl.BlockSpec(w1r.shape, lambda b: (0, 0, 0, 0)),
           pl.BlockSpec(per.shape, lambda b: (0, 0, 0, 0))],
        out_specs=[pl.BlockSpec((nch, width), lambda b: (b, 0))] * 2,
        compiler_params=_params(("parallel",)),
        name="cmp_stage1_prompt",
    )(*([kv_f32] * (2 * N_KV)), w1r, per)
    return _cmp_stage2(top, bot, w2, kn0, nch)


def _compress_sample(cache2d, pt_flat, w1r, per, w2, kn0, *, dec_batch, n_pages, page_size):
    width = 2 * N_KV * HEAD_DIM
    pages_per_step = _pick(n_pages, (16, 8, 4, 2, 1))
    n_groups = n_pages // pages_per_step
    rows_step = pages_per_step * page_size
    nch_step = rows_step // STRIDE
    nch = n_pages * page_size // STRIDE
    kern = functools.partial(_cmp1_sample_kernel, pages_per_step=pages_per_step, n_pages=n_pages,
                             page_size=page_size, n_groups=n_groups, n_steps=dec_batch * n_groups)
    top, bot = pl.pallas_call(
        kern,
        out_shape=[jax.ShapeDtypeStruct((dec_batch * nch, width), F32)] * 2,
        grid_spec=pltpu.PrefetchScalarGridSpec(
            num_scalar_prefetch=1,
            grid=(dec_batch, n_groups),
            in_specs=[pl.BlockSpec(memory_space=pl.ANY),
                      pl.BlockSpec(w1r.shape, lambda b, g, pt: (0, 0, 0, 0)),
                      pl.BlockSpec(per.shape, lambda b, g, pt: (0, 0, 0, 0))],
            out_specs=[pl.BlockSpec((nch_step, width), lambda b, g, pt: (b * n_groups + g, 0))] * 2,
            scratch_shapes=[pltpu.VMEM((2, 2 * N_KV, rows_step, HEAD_DIM), F32),
                            pltpu.SemaphoreType.DMA((2,))]),
        compiler_params=_params(("arbitrary", "arbitrary")),
        name="cmp_stage1_sample",
    )(pt_flat, cache2d, w1r, per)
    return _cmp_stage2(top, bot, w2, kn0, nch)


def _split3(x):
    hi = x.astype(BF16)
    r1 = x - hi.astype(F32)
    mid = r1.astype(BF16)
    lo = (r1 - mid.astype(F32)).astype(BF16)
    return hi, mid, lo


def _dot_f32ish(x, m_bf16):
    hi, mid, lo = _split3(x)
    out = jnp.dot(lo, m_bf16, preferred_element_type=F32)
    out = out + jnp.dot(mid, m_bf16, preferred_element_type=F32)
    return out + jnp.dot(hi, m_bf16, preferred_element_type=F32)


def _inter_t(n_c, n_s_lanes):
    c0 = lax.broadcasted_iota(jnp.int32, (n_c, n_s_lanes), 0) * STRIDE
    s0 = lax.broadcasted_iota(jnp.int32, (n_c, n_s_lanes), 1) * L_SLC
    return ((c0 < s0 + L_SLC) & (c0 + L_CMP > s0)).astype(BF16)


def _rank_desc(score, ns):
    lane = lax.broadcasted_iota(jnp.int32, score.shape, 1)
    rank = jnp.zeros(score.shape, jnp.int32)
    for sp in range(ns):
        col = score[:, sp:sp + 1]
        ahead = (col > score) | ((col == score) & (lane > sp))
        rank = rank + ahead.astype(jnp.int32)
    return rank


def _selection_scores(imp, qblk, ns):
    j = lax.broadcasted_iota(jnp.int32, imp.shape, 1)
    forced = (j == 0) | (j == qblk) | (j == qblk - 1)
    score = jnp.where(forced, FORCE, imp)
    score = jnp.where(j <= qblk, score, -FORCE)
    return jnp.where(j < ns, score, -2.0 * FORCE)


def _flash_init(m_sc, l_sc, acc_sc):
    m_sc[...] = jnp.full(m_sc.shape, M_FLOOR, F32)
    l_sc[...] = jnp.zeros(l_sc.shape, F32)
    acc_sc[...] = jnp.zeros(acc_sc.shape, F32)


def _flash_tile(q, kt, vt, *, s_sc, p_sc, bias_sc, m_sc, l_sc, alpha_sc, acc_sc, tq, width):
    c = (HEAD_DIM ** -0.5) * LOG2_E
    nt = (((1,), (1,)), ((), ()))
    reps = width // LANES
    s_sc[:, :width] = lax.dot_general(q, kt, nt, preferred_element_type=F32)
    slabs = [(slice(r0, r0 + FLASH_SLAB_ROWS), slice(r0 % tq, r0 % tq + FLASH_SLAB_ROWS))
             for r0 in range(0, q.shape[0], FLASH_SLAB_ROWS)]
    for rs, qs in slabs:
        s = s_sc[rs, :width] * c + bias_sc[qs, :width]
        s_sc[rs, :width] = s
        m_old = m_sc[rs, :]
        m_new = jnp.maximum(m_old, jnp.max(s, axis=-1, keepdims=True))
        alpha_sc[rs, :] = jnp.exp2(m_old - m_new)
        m_sc[rs, :] = m_new
    for rs, qs in slabs:
        m_b = jnp.concatenate([m_sc[rs, :]] * reps, axis=1)
        p = jnp.exp2(s_sc[rs, :width] - m_b)
        l_sc[rs, :] = alpha_sc[rs, :] * l_sc[rs, :] + jnp.sum(p, axis=-1, keepdims=True)
        p_sc[rs, :width] = p.astype(p_sc.dtype)
    pv = jnp.dot(p_sc[:, :width], vt, preferred_element_type=F32)
    acc_sc[...] = alpha_sc[...] * acc_sc[...] + pv


def _nsa_prompt_kernel(q_ref, g_ref, kc_ref, vc_ref, ks_ref, vs_ref, kw_ref, vw_ref, o_init_ref, o_ref,
                       s_sc, p_sc, bias_sc, m_sc, l_sc, alpha_sc, acc_sc, *, tq, tk, hpg, seq, ncp):
    del o_init_ref
    qt = pl.program_id(2)
    t0 = qt * tq
    scale = HEAD_DIM ** -0.5
    rows = hpg * tq
    qf = q_ref[...]
    q = jnp.concatenate([qf[:, h * HEAD_DIM:(h + 1) * HEAD_DIM] for h in range(hpg)], axis=0)
    qpos = t0 + lax.broadcasted_iota(jnp.int32, (tq, 1), 0)
    nt = (((1,), (1,)), ((), ()))

    s = lax.dot_general(q, kc_ref[...], nt, preferred_element_type=F32) * scale
    c_end = lax.broadcasted_iota(jnp.int32, (tq, ncp), 1) * STRIDE + (L_CMP - 1)
    mc = (c_end <= qpos)[None]
    s3 = jnp.where(mc, s.reshape(hpg, tq, ncp), NEG)
    e = jnp.where(mc, jnp.exp(s3 - jnp.max(s3, axis=-1, keepdims=True)), 0.0)
    den = jnp.sum(e, axis=-1, keepdims=True)
    p_c = jnp.where(den > 0.0, e / jnp.where(den > 0.0, den, 1.0), 0.0)
    o_c = jnp.dot(p_c.reshape(rows, ncp).astype(BF16), vc_ref[...], preferred_element_type=F32)
    psum = jnp.sum(p_c, axis=0)

    ns = -(-seq // L_SLC)
    imp = _dot_f32ish(psum, _inter_t(ncp, LANES))
    qblk = _div_pow2(qpos, L_SLC)
    score = _selection_scores(imp, qblk, ns)
    rank = _rank_desc(score, ns)
    sel = ((rank < N_SEL) & (score > -0.5 * FORCE)).astype(BF16)

    flash = functools.partial(_flash_tile, s_sc=s_sc, p_sc=p_sc, bias_sc=bias_sc, m_sc=m_sc, l_sc=l_sc,
                              alpha_sc=alpha_sc, acc_sc=acc_sc, tq=tq)
    _flash_init(m_sc, l_sc, acc_sc)

    def body(j, carry):
        k0 = pl.multiple_of(j * tk, tk)
        kpos = k0 + lax.broadcasted_iota(jnp.int32, (1, tk), 1)
        expand = (lax.broadcasted_iota(jnp.int32, (LANES, tk), 0) == _div_pow2(kpos, L_SLC)).astype(BF16)
        km = jnp.dot(sel, expand, preferred_element_type=F32) > 0.5
        bias_sc[:, :tk] = jnp.where(km & (kpos <= qpos), 0.0, NEG)
        flash(q, ks_ref[pl.ds(k0, tk), :], vs_ref[pl.ds(k0, tk), :], width=tk)
        return carry

    lax.fori_loop(0, (t0 + tq - 1) // tk + 1, body, 0)
    o_s = acc_sc[...] / l_sc[...]

    wl = WINDOW + tq
    w0 = pl.multiple_of(jnp.clip(t0 - WINDOW, 0, seq - wl), tq)
    wpos = w0 + lax.broadcasted_iota(jnp.int32, (1, wl), 1)
    dlt = qpos - wpos
    bias_sc[...] = jnp.where((dlt >= 0) & (dlt <= WINDOW), 0.0, NEG)
    _flash_init(m_sc, l_sc, acc_sc)
    flash(q, kw_ref[pl.ds(w0, wl), :], vw_ref[pl.ds(w0, wl), :], width=wl)
    o_w = acc_sc[...] / l_sc[...]

    gt = g_ref[...]
    for h in range(hpg):
        sl = slice(h * tq, (h + 1) * tq)
        g0 = gt[:, 0 * hpg + h:0 * hpg + h + 1]
        g1 = gt[:, 1 * hpg + h:1 * hpg + h + 1]
        g2 = gt[:, 2 * hpg + h:2 * hpg + h + 1]
        o_ref[:, h * HEAD_DIM:(h + 1) * HEAD_DIM] = (
            g0 * o_c[sl, :] + g1 * o_s[sl, :] + g2 * o_w[sl, :]).astype(o_ref.dtype)


def _nsa_prompt(q, gates, kc, kv_bf, *, m_total, batch, seq, hpg):
    d = q.shape[1]
    tq = 256
    assert seq % tq == 0 and seq >= WINDOW + tq
    tk = _pick(seq, (512, 256, 128))
    ncp = seq // STRIDE
    nqt = seq // tq
    gw = hpg * HEAD_DIM
    rows = hpg * tq
    wmax = max(tk, WINDOW + tq)
    kern = functools.partial(_nsa_prompt_kernel, tq=tq, tk=tk, hpg=hpg, seq=seq, ncp=ncp)

    def kvspec(sec):
        return pl.BlockSpec((seq, HEAD_DIM), lambda b, g, t, sec=sec: (b, sec * N_KV + g))

    return pl.pallas_call(
        kern,
        out_shape=jax.ShapeDtypeStruct((m_total, d), BF16),
        grid=(batch, N_KV, nqt),
        in_specs=[pl.BlockSpec((tq, gw), lambda b, g, t: (b * nqt + t, g)),
                  pl.BlockSpec((tq, LANES), lambda b, g, t: (b * nqt + t, g)),
                  pl.BlockSpec((ncp, HEAD_DIM), lambda b, g, t: (b, g)),
                  pl.BlockSpec((ncp, HEAD_DIM), lambda b, g, t: (b, N_KV + g)),
                  kvspec(2), kvspec(3), kvspec(4), kvspec(5),
                  pl.BlockSpec(memory_space=pl.ANY)],
        out_specs=pl.BlockSpec((tq, gw), lambda b, g, t: (b * nqt + t, g)),
        scratch_shapes=[pltpu.VMEM((rows, wmax), F32),
                        pltpu.VMEM((rows, wmax), BF16),
                        pltpu.VMEM((tq, wmax), F32),
                        pltpu.VMEM((rows, LANES), F32), pltpu.VMEM((rows, LANES), F32),
                        pltpu.VMEM((rows, LANES), F32), pltpu.VMEM((rows, HEAD_DIM), F32)],
        input_output_aliases={8: 0},
        compiler_params=_params(("parallel", "parallel", "arbitrary")),
        name="nsa_prompt",
    )(q, gates, kc, kc, kv_bf, kv_bf, kv_bf, kv_bf, jnp.zeros((m_total, d), BF16))


def _nsa_sample_select_kernel(q_ref, kc_ref, vc_ref, oc_ref, idx_ref, *, dec_batch, dec_seq, hpg,
                              past_len, ncp, ns, ns_lanes):
    scale = HEAD_DIM ** -0.5
    nt = (((1,), (1,)), ((), ()))
    qf = q_ref[...].astype(F32)
    rows = hpg * dec_seq
    qrow = lax.broadcasted_iota(jnp.int32, (dec_seq, 1), 0)
    qpos = past_len + qrow
    c_end = lax.broadcasted_iota(jnp.int32, (dec_seq, ncp), 1) * STRIDE + (L_CMP - 1)
    mc = (c_end <= qpos)[None]
    psums = []
    for b in range(dec_batch):
        qb = qf[b * dec_seq:(b + 1) * dec_seq, :]
        q = jnp.concatenate([qb[:, h * HEAD_DIM:(h + 1) * HEAD_DIM] for h in range(hpg)],
                            axis=0).astype(BF16)
        kc = kc_ref[b * ncp:(b + 1) * ncp, :]
        vc = vc_ref[b * ncp:(b + 1) * ncp, :]
        s = lax.dot_general(q, kc, nt, preferred_element_type=F32) * scale
        s3 = jnp.where(mc, s.reshape(hpg, dec_seq, ncp), NEG)
        e = jnp.where(mc, jnp.exp(s3 - jnp.max(s3, axis=-1, keepdims=True)), 0.0)
        den = jnp.sum(e, axis=-1, keepdims=True)
        p_c = jnp.where(den > 0.0, e / jnp.where(den > 0.0, den, 1.0), 0.0)
        o_c = jnp.dot(p_c.reshape(rows, ncp).astype(BF16), vc, preferred_element_type=F32)
        for h in range(hpg):
            oc_ref[b * dec_seq:(b + 1) * dec_seq, h * HEAD_DIM:(h + 1) * HEAD_DIM] = (
                o_c[h * dec_seq:(h + 1) * dec_seq, :])
        psums.append(jnp.sum(p_c, axis=0))
    psum = jnp.concatenate(psums, axis=0)
    imp = _dot_f32ish(psum, _inter_t(ncp, ns_lanes))
    n_rows = dec_batch * dec_seq
    qpos_all = past_len + (lax.broadcasted_iota(jnp.int32, (n_rows, 1), 0) & (dec_seq - 1))
    score = _selection_scores(imp, _div_pow2(qpos_all, L_SLC), ns)
    rank = _rank_desc(score, ns)
    ok = score > -0.5 * FORCE
    lane = lax.broadcasted_iota(jnp.int32, score.shape, 1)
    out_lane = lax.broadcasted_iota(jnp.int32, (n_rows, LANES), 1)
    idx = jnp.full((n_rows, LANES), -1, jnp.int32)
    for k in range(N_SEL):
        hit = (rank == k) & ok
        blk = jnp.sum(jnp.where(hit, (lane + 1).astype(F32), 0.0), axis=-1, keepdims=True)
        blk = blk.astype(jnp.int32) - 1
        idx = jnp.where(out_lane == k, blk, idx)
    idx_ref[...] = idx


def _nsa_sample_attend_kernel(idx_ref, pt_ref, q_ref, g_ref, oc_ref, knew_ref, vnew_ref, kwo_ref, vwo_ref,
                              kwn_ref, vwn_ref, cache_ref, o_in_ref, o_ref, kbuf, vbuf, of32, sems, *,
                              dec_batch, dec_seq, hpg, past_len, n_pages, page_size, l_win):
    del o_in_ref
    g = pl.program_id(0)
    scale = HEAD_DIM ** -0.5
    nt = (((1,), (1,)), ((), ()))
    rows = hpg * dec_seq
    n_past_blocks = past_len // L_SLC
    blocks_per_page = page_size // L_SLC
    nkeys = N_SEL * L_SLC
    qf = q_ref[...].astype(F32)
    gt = g_ref[...]
    qrow = lax.broadcasted_iota(jnp.int32, (rows, 1), 0) & (dec_seq - 1)
    key_slot = _div_pow2(lax.broadcasted_iota(jnp.int32, (1, nkeys), 1), L_SLC)
    newj = lax.broadcasted_iota(jnp.int32, (1, dec_seq), 1)

    for b in range(dec_batch):
        def copies(qi, k):
            blk = idx_ref[((g * dec_batch + b) * dec_seq + qi) * N_SEL + k]
            blk = jnp.clip(blk, 0, n_past_blocks - 1)
            page = pt_ref[b * n_pages + blk // blocks_per_page]
            row0 = pl.multiple_of((blk % blocks_per_page) * L_SLC, L_SLC)
            ck = pltpu.make_async_copy(cache_ref.at[page, pl.ds(row0, L_SLC), 2, g, :],
                                       kbuf.at[qi, pl.ds(k * L_SLC, L_SLC), :], sems.at[0])
            cv = pltpu.make_async_copy(cache_ref.at[page, pl.ds(row0, L_SLC), 3, g, :],
                                       vbuf.at[qi, pl.ds(k * L_SLC, L_SLC), :], sems.at[1])
            return ck, cv

        for qi in range(dec_seq):
            for k in range(N_SEL):
                ck, cv = copies(qi, k)
                ck.start()
                cv.start()
        for qi in range(dec_seq):
            for k in range(N_SEL):
                ck, cv = copies(qi, k)
                ck.wait()
                cv.wait()

        qb = qf[b * dec_seq:(b + 1) * dec_seq, :]
        q = jnp.concatenate([qb[:, h * HEAD_DIM:(h + 1) * HEAD_DIM] for h in range(hpg)],
                            axis=0).astype(BF16)
        knew = knew_ref[b * dec_seq:(b + 1) * dec_seq, :].astype(BF16)
        vnew = vnew_ref[b * dec_seq:(b + 1) * dec_seq, :].astype(BF16)
        s_new = lax.dot_general(q, knew, nt, preferred_element_type=F32) * scale

        o_s = jnp.zeros((rows, HEAD_DIM), F32)
        for qi in range(dec_seq):
            valid = jnp.zeros((1, nkeys), jnp.int32)
            has_new = jnp.int32(0)
            for k in range(N_SEL):
                blk = idx_ref[((g * dec_batch + b) * dec_seq + qi) * N_SEL + k]
                is_past = ((blk >= 0) & (blk < n_past_blocks)).astype(jnp.int32)
                valid = jnp.where(key_slot == k, is_past, valid)
                has_new = has_new | (blk == n_past_blocks).astype(jnp.int32)
            mk = valid > 0
            mn = ((newj <= qi).astype(jnp.int32) * has_new) > 0
            kq = kbuf[qi].astype(BF16)
            vq = vbuf[qi].astype(BF16)
            s_old = lax.dot_general(q, kq, nt, preferred_element_type=F32) * scale
            s_old = jnp.where(mk, s_old, NEG)
            s_n = jnp.where(mn, s_new, NEG)
            mx = jnp.maximum(jnp.max(s_old, axis=-1, keepdims=True), jnp.max(s_n, axis=-1, keepdims=True))
            p_old = jnp.where(mk, jnp.exp(s_old - mx), 0.0)
            p_n = jnp.where(mn, jnp.exp(s_n - mx), 0.0)
            den = jnp.sum(p_old, axis=-1, keepdims=True) + jnp.sum(p_n, axis=-1, keepdims=True)
            o = (jnp.dot(p_old.astype(BF16), vq, preferred_element_type=F32)
                 + jnp.dot(p_n.astype(BF16), vnew, preferred_element_type=F32)) / den
            o_s = jnp.where(qrow == qi, o, o_s)

        kwo = kwo_ref[b * l_win:(b + 1) * l_win, :].astype(BF16)
        vwo = vwo_ref[b * l_win:(b + 1) * l_win, :].astype(BF16)
        kwn = kwn_ref[b * dec_seq:(b + 1) * dec_seq, :].astype(BF16)
        vwn = vwn_ref[b * dec_seq:(b + 1) * dec_seq, :].astype(BF16)
        qpos = past_len + qrow
        wpos_o = past_len - l_win + lax.broadcasted_iota(jnp.int32, (1, l_win), 1)
        wpos_n = past_len + newj
        d_o = qpos - wpos_o
        d_n = qpos - wpos_n
        m_o = (d_o >= 0) & (d_o <= WINDOW) & (wpos_o >= 0)
        m_n = (d_n >= 0) & (d_n <= WINDOW)
        sw_o = jnp.where(m_o, lax.dot_general(q, kwo, nt, preferred_element_type=F32) * scale, NEG)
        sw_n = jnp.where(m_n, lax.dot_general(q, kwn, nt, preferred_element_type=F32) * scale, NEG)
        mx = jnp.maximum(jnp.max(sw_o, axis=-1, keepdims=True), jnp.max(sw_n, axis=-1, keepdims=True))
        pw_o = jnp.where(m_o, jnp.exp(sw_o - mx), 0.0)
        pw_n = jnp.where(m_n, jnp.exp(sw_n - mx), 0.0)
        den = jnp.sum(pw_o, axis=-1, keepdims=True) + jnp.sum(pw_n, axis=-1, keepdims=True)
        o_w = (jnp.dot(pw_o.astype(BF16), vwo, preferred_element_type=F32)
               + jnp.dot(pw_n.astype(BF16), vwn, preferred_element_type=F32)) / den

        gb = gt[b * dec_seq:(b + 1) * dec_seq, :]
        for h in range(hpg):
            sl = slice(h * dec_seq, (h + 1) * dec_seq)
            g0 = gb[:, 0 * hpg + h:0 * hpg + h + 1]
            g1 = gb[:, 1 * hpg + h:1 * hpg + h + 1]
            g2 = gb[:, 2 * hpg + h:2 * hpg + h + 1]
            o_c = oc_ref[b * dec_seq:(b + 1) * dec_seq, h * HEAD_DIM:(h + 1) * HEAD_DIM]
            of32[b * dec_seq:(b + 1) * dec_seq, h * HEAD_DIM:(h + 1) * HEAD_DIM] = (
                g0 * o_c + g1 * o_s[sl, :] + g2 * o_w[sl, :])
    o_ref[...] = of32[...].astype(o_ref.dtype)


def _nsa_sample(q, gates, kc_s, kv_f32, cache2d, win2d, pt_flat, o_prompt, *, m_prompt, dec_batch,
                dec_seq, hpg, past_len, n_pages, page_size, l_win):
    ms = dec_batch * dec_seq
    gw = hpg * HEAD_DIM
    ncp = past_len // STRIDE
    t_all = past_len + dec_seq
    ns = -(-t_all // L_SLC)
    ns_lanes = -(-ns // LANES) * LANES
    rb = m_prompt // ms
    sel = functools.partial(_nsa_sample_select_kernel, dec_batch=dec_batch, dec_seq=dec_seq, hpg=hpg,
                            past_len=past_len, ncp=ncp, ns=ns, ns_lanes=ns_lanes)
    oc, idx = pl.pallas_call(
        sel,
        out_shape=[jax.ShapeDtypeStruct((ms, N_KV * gw), F32),
                   jax.ShapeDtypeStruct((N_KV * ms, LANES), jnp.int32)],
        grid=(N_KV,),
        in_specs=[pl.BlockSpec((ms, gw), lambda g: (rb, g)),
                  pl.BlockSpec((dec_batch * ncp, HEAD_DIM), lambda g: (0, g)),
                  pl.BlockSpec((dec_batch * ncp, HEAD_DIM), lambda g: (0, N_KV + g))],
        out_specs=[pl.BlockSpec((ms, gw), lambda g: (0, g)),
                   pl.BlockSpec((ms, LANES), lambda g: (g, 0))],
        compiler_params=_params(("parallel",)),
        name="nsa_sample_select",
    )(q, kc_s, kc_s)
    idx_flat = idx[:, :N_SEL].reshape(-1)

    att = functools.partial(_nsa_sample_attend_kernel, dec_batch=dec_batch, dec_seq=dec_seq, hpg=hpg,
                            past_len=past_len, n_pages=n_pages, page_size=page_size, l_win=l_win)

    def sm(f):
        return lambda g, idx_r, pt_r: f(g)

    return pl.pallas_call(
        att,
        out_shape=jax.ShapeDtypeStruct(o_prompt.shape, o_prompt.dtype),
        grid_spec=pltpu.PrefetchScalarGridSpec(
            num_scalar_prefetch=2,
            grid=(N_KV,),
            in_specs=[pl.BlockSpec((ms, gw), sm(lambda g: (rb, g))),
                      pl.BlockSpec((ms, LANES), sm(lambda g: (rb, g))),
                      pl.BlockSpec((ms, gw), sm(lambda g: (0, g))),
                      pl.BlockSpec((ms, HEAD_DIM), sm(lambda g: (rb, 2 * N_KV + g))),
                      pl.BlockSpec((ms, HEAD_DIM), sm(lambda g: (rb, 3 * N_KV + g))),
                      pl.BlockSpec((dec_batch * l_win, HEAD_DIM), sm(lambda g: (0, g))),
                      pl.BlockSpec((dec_batch * l_win, HEAD_DIM), sm(lambda g: (0, N_KV + g))),
                      pl.BlockSpec((ms, HEAD_DIM), sm(lambda g: (rb, 4 * N_KV + g))),
                      pl.BlockSpec((ms, HEAD_DIM), sm(lambda g: (rb, 5 * N_KV + g))),
                      pl.BlockSpec(memory_space=pl.ANY),
                      pl.BlockSpec(memory_space=pl.ANY)],
            out_specs=pl.BlockSpec((ms, gw), sm(lambda g: (rb, g))),
            scratch_shapes=[pltpu.VMEM((dec_seq, N_SEL * L_SLC, HEAD_DIM), F32),
                            pltpu.VMEM((dec_seq, N_SEL * L_SLC, HEAD_DIM), F32),
                            pltpu.VMEM((ms, gw), F32),
                            pltpu.SemaphoreType.DMA((2,))]),
        input_output_aliases={12: 0},
        compiler_params=_params(("arbitrary",)),
        name="nsa_sample_attend",
    )(idx_flat, pt_flat, q, gates, oc, kv_f32, kv_f32, win2d, win2d, kv_f32, kv_f32, cache2d, o_prompt)


def _router_kernel(x_ref, g_ref, r_ref, hp_ref, idx_ref, wts_ref, *, n_experts):
    x = x_ref[...]
    h = x * lax.rsqrt(jnp.mean(x * x, axis=-1, keepdims=True) + EPS) * g_ref[...]
    half = h.shape[1] // 2
    bits = pltpu.bitcast(h.astype(BF16).astype(F32), jnp.uint32)
    hp_ref[...] = (bits[:, :half] >> 16) | (bits[:, half:] & jnp.uint32(0xFFFF0000))
    r_hi, r_mid, r_lo = _split3(r_ref[...])
    h_hi, h_mid, h_lo = _split3(h)

    def d(a, b):
        return jnp.dot(a, b, preferred_element_type=F32)

    logits = (d(h_lo, r_hi) + d(h_hi, r_lo) + d(h_mid, r_mid)) + (d(h_mid, r_hi) + d(h_hi, r_mid)) + d(h_hi, r_hi)
    lane = lax.broadcasted_iota(jnp.int32, logits.shape, 1)
    logits = jnp.where(lane < n_experts, logits, -jnp.inf)
    v1 = jnp.max(logits, axis=-1, keepdims=True)
    i1 = jnp.min(jnp.where(logits == v1, lane, LANES), axis=-1, keepdims=True)
    rest = jnp.where(lane == i1, -jnp.inf, logits)
    v2 = jnp.max(rest, axis=-1, keepdims=True)
    i2 = jnp.min(jnp.where(rest == v2, lane, LANES), axis=-1, keepdims=True)
    e2 = jnp.exp(v2 - v1)
    w1 = 1.0 / (1.0 + e2)
    w2 = e2 / (1.0 + e2)
    idx_ref[...] = jnp.where(lane == 0, i1, jnp.where(lane == 1, i2, 0))
    wts_ref[...] = jnp.where(lane == 0, w1, jnp.where(lane == 1, w2, 0.0))


def _router(x, gain, router_p, n_experts):
    m, d = x.shape
    tm = _pick(m, (192, 128, 64, 32, 16))
    return pl.pallas_call(
        functools.partial(_router_kernel, n_experts=n_experts),
        out_shape=[jax.ShapeDtypeStruct((m, d // 2), jnp.uint32),
                   jax.ShapeDtypeStruct((m, LANES), jnp.int32),
                   jax.ShapeDtypeStruct((m, LANES), F32)],
        grid=(m // tm,),
        in_specs=[pl.BlockSpec((tm, d), lambda i: (i, 0)),
                  pl.BlockSpec((1, d), lambda i: (0, 0)),
                  pl.BlockSpec((d, LANES), lambda i: (0, 0))],
        out_specs=[pl.BlockSpec((tm, d // 2), lambda i: (i, 0)),
                   pl.BlockSpec((tm, LANES), lambda i: (i, 0)),
                   pl.BlockSpec((tm, LANES), lambda i: (i, 0))],
        compiler_params=_params(("parallel",)),
        name="moe_router",
    )(x, gain, router_p)


def _moe_plan(idx2, n_experts, rc, nch, sub):
    e = idx2.reshape(-1)
    onehot = (e[:, None] == jnp.arange(n_experts, dtype=jnp.int32)[None, :]).astype(jnp.int32)
    csum = jnp.cumsum(onehot, axis=0)
    rank = jnp.take_along_axis(csum, e[:, None], axis=1)[:, 0] - 1
    counts = csum[-1]
    nchunks = (counts + rc - 1) // rc
    cend = jnp.cumsum(nchunks)
    cstart = cend - nchunks
    pos = (cstart[e] + rank // rc) * rc + rank % rc
    n_active = cend[-1]
    c = jnp.arange(nch, dtype=jnp.int32)
    cx = jnp.minimum(c, n_active - 1)
    ce = jnp.sum((cend[None, :] <= cx[:, None]).astype(jnp.int32), axis=1)
    rows = jnp.clip(counts[ce] - (cx - cstart[ce]) * rc, 0, rc)
    nsub = jnp.where(c < n_active, (rows + sub - 1) // sub, 0)
    return pos.astype(jnp.int32), nsub.astype(jnp.int32), cx.astype(jnp.int32), ce


def _moe_scatter_kernel(pos_ref, hp_ref, xs_in_ref, xs_ref, sems, *, tt):
    del xs_in_ref

    def copies(r):
        return [pltpu.make_async_copy(hp_ref.at[pl.ds(r, 1), :],
                                      xs_ref.at[pl.ds(pos_ref[0, 0, TOP_K * r + s], 1), :], sems.at[s])
                for s in range(TOP_K)]

    def start(r, carry):
        for cp in copies(r):
            cp.start()
        return carry

    def wait(r, carry):
        for cp in copies(r):
            cp.wait()
        return carry

    lax.fori_loop(0, tt, start, 0)
    lax.fori_loop(0, tt, wait, 0)


def _moe_scatter(hp, pos, n_rows):
    m, dp = hp.shape
    tt = _pick(m, (192, 128, 64, 32, 16))
    pos3 = pos.reshape(m // tt, 1, TOP_K * tt)
    return pl.pallas_call(
        functools.partial(_moe_scatter_kernel, tt=tt),
        out_shape=jax.ShapeDtypeStruct((n_rows, dp), hp.dtype),
        grid=(m // tt,),
        in_specs=[pl.BlockSpec((1, 1, TOP_K * tt), lambda i: (i, 0, 0), memory_space=pltpu.SMEM),
                  pl.BlockSpec((tt, dp), lambda i: (i, 0)),
                  pl.BlockSpec(memory_space=pl.ANY)],
        out_specs=pl.BlockSpec(memory_space=pl.ANY),
        scratch_shapes=[pltpu.SemaphoreType.DMA((TOP_K,))],
        input_output_aliases={2: 0},
        compiler_params=_params(("arbitrary",)),
        name="moe_scatter",
    )(pos3, hp, jnp.zeros((n_rows, dp), hp.dtype))


def _unpack_bf16_pair(xp):
    lo = pltpu.bitcast(xp << 16, F32).astype(BF16)
    hi = pltpu.bitcast(xp & jnp.uint32(0xFFFF0000), F32).astype(BF16)
    return lo, hi


def _over_valid_rows(ns, rc, sub, body):
    big = 8 * sub
    for b in range(rc // big):
        @pl.when(ns >= 8 * (b + 1))
        def _(b=b):
            body(pl.ds(b * big, big))

    off = (ns // 8) * big
    rem = ns % 8
    for bit in (4, 2, 1):
        @pl.when((rem & bit) != 0)
        def _(bit=bit, off=off):
            body(pl.ds(pl.multiple_of(off, sub), bit * sub))

        off = off + (rem & bit) * sub


def _moe_gu_kernel(nsub_ref, cx_ref, ce_ref, x_ref, wgl_ref, wgh_ref, wul_ref, wuh_ref, o_ref,
                   accg, accu, wb, *, nk, sub):
    del cx_ref, ce_ref
    c = pl.program_id(0)
    k = pl.program_id(2)
    ns = nsub_ref[c]

    @pl.when(ns > 0)
    def _():
        @pl.when(k == 0)
        def _():
            accg[...] = jnp.zeros_like(accg)
            accu[...] = jnp.zeros_like(accu)

        for n, w_ref in enumerate((wgl_ref, wgh_ref, wul_ref, wuh_ref)):
            wb[n] = w_ref[...].astype(BF16)

        def body(rows):
            lo, hi = _unpack_bf16_pair(x_ref[rows, :])
            accg[rows, :] += (jnp.dot(lo, wb[0], preferred_element_type=F32)
                              + jnp.dot(hi, wb[1], preferred_element_type=F32))
            accu[rows, :] += (jnp.dot(lo, wb[2], preferred_element_type=F32)
                              + jnp.dot(hi, wb[3], preferred_element_type=F32))

        _over_valid_rows(ns, accg.shape[0], sub, body)

        @pl.when(k == nk - 1)
        def _():
            g = accg[...]
            o_ref[...] = (g * _sigmoid(g) * accu[...]).astype(o_ref.dtype)


def _moe_down_kernel(nsub_ref, cx_ref, ce_ref, x_ref, w_ref, o_ref, wb, *, sub):
    del cx_ref, ce_ref
    c = pl.program_id(0)
    k = pl.program_id(2)
    ns = nsub_ref[c]

    @pl.when(ns > 0)
    def _():
        @pl.when(k == 0)
        def _():
            o_ref[...] = jnp.zeros_like(o_ref)

        wb[...] = w_ref[...].astype(BF16)

        def body(rows):
            o_ref[rows, :] += jnp.dot(x_ref[rows, :], wb[...], preferred_element_type=F32)

        _over_valid_rows(ns, o_ref.shape[0], sub, body)


def _moe_experts(xs, wgu, wdn, nsub, cx, ce, *, d, d_ff, rc, nch, sub):
    half = d // 2
    tkp = _pick(half, (512, 256, 128))
    nk = half // tkp
    tn = _pick(d_ff, (512, 256, 128))
    nj = d_ff // tn

    def frozen(c, a, last, nsub_r):
        return jnp.where(nsub_r[c] > 0, a, last)

    def x_map(c, j, k, nsub_r, cx_r, ce_r):
        return (cx_r[c], frozen(c, k, nk - 1, nsub_r))

    def w_map(col_off, row_off):
        def f(c, j, k, nsub_r, cx_r, ce_r):
            return (ce_r[c] * (d // tkp) + row_off // tkp + frozen(c, k, nk - 1, nsub_r),
                    col_off // tn + frozen(c, j, nj - 1, nsub_r))
        return f

    def o_map(c, j, k, nsub_r, cx_r, ce_r):
        return (cx_r[c], frozen(c, j, nj - 1, nsub_r))

    act = pl.pallas_call(
        functools.partial(_moe_gu_kernel, nk=nk, sub=sub),
        out_shape=jax.ShapeDtypeStruct((nch * rc, d_ff), BF16),
        grid_spec=pltpu.PrefetchScalarGridSpec(
            num_scalar_prefetch=3,
            grid=(nch, nj, nk),
            in_specs=[pl.BlockSpec((rc, tkp), x_map),
                      pl.BlockSpec((tkp, tn), w_map(0, 0)),
                      pl.BlockSpec((tkp, tn), w_map(0, half)),
                      pl.BlockSpec((tkp, tn), w_map(d_ff, 0)),
                      pl.BlockSpec((tkp, tn), w_map(d_ff, half))],
            out_specs=pl.BlockSpec((rc, tn), o_map),
            scratch_shapes=[pltpu.VMEM((rc, tn), F32), pltpu.VMEM((rc, tn), F32),
                            pltpu.VMEM((4, tkp, tn), BF16)]),
        compiler_params=_params(("arbitrary", "arbitrary", "arbitrary")),
        name="moe_gu",
    )(nsub, cx, ce, xs, wgu, wgu, wgu, wgu)

    tk2 = _pick(d_ff, (1024, 512, 256, 128))
    nk2 = d_ff // tk2
    tn2 = _pick(d, (1024, 512, 256, 128))
    nj2 = d // tn2

    def x2_map(c, j, k, nsub_r, cx_r, ce_r):
        return (cx_r[c], frozen(c, k, nk2 - 1, nsub_r))

    def w2_map(c, j, k, nsub_r, cx_r, ce_r):
        return (ce_r[c] * nk2 + frozen(c, k, nk2 - 1, nsub_r), frozen(c, j, nj2 - 1, nsub_r))

    def o2_map(c, j, k, nsub_r, cx_r, ce_r):
        return (cx_r[c], frozen(c, j, nj2 - 1, nsub_r))

    return pl.pallas_call(
        functools.partial(_moe_down_kernel, sub=sub),
        out_shape=jax.ShapeDtypeStruct((nch * rc, d), F32),
        grid_spec=pltpu.PrefetchScalarGridSpec(
            num_scalar_prefetch=3,
            grid=(nch, nj2, nk2),
            in_specs=[pl.BlockSpec((rc, tk2), x2_map),
                      pl.BlockSpec((tk2, tn2), w2_map)],
            out_specs=pl.BlockSpec((rc, tn2), o2_map),
            scratch_shapes=[pltpu.VMEM((tk2, tn2), BF16)]),
        compiler_params=_params(("arbitrary", "arbitrary", "arbitrary")),
        name="moe_down",
    )(nsub, cx, ce, act, wdn)


def _moe_combine_kernel(pos_ref, x_ref, wts_ref, y_ref, o_ref, ybuf, sems, *, tt):
    i = pl.program_id(0)
    del i

    def copies(r):
        return [pltpu.make_async_copy(y_ref.at[pl.ds(pos_ref[0, 0, TOP_K * r + s], 1), :],
                                      ybuf.at[s, pl.ds(r, 1), :], sems.at[s])
                for s in range(TOP_K)]

    def start(r, carry):
        for cp in copies(r):
            cp.start()
        return carry

    def wait(r, carry):
        for cp in copies(r):
            cp.wait()
        return carry

    lax.fori_loop(0, tt, start, 0)
    lax.fori_loop(0, tt, wait, 0)
    w = wts_ref[...]
    o_ref[...] = x_ref[...] + (w[:, 0:1] * ybuf[0] + w[:, 1:2] * ybuf[1])


def _moe_combine(x, wts, y, pos):
    m, d = x.shape
    tt = _pick(m, (192, 128, 64, 32, 16))
    pos3 = pos.reshape(m // tt, 1, TOP_K * tt)
    return pl.pallas_call(
        functools.partial(_moe_combine_kernel, tt=tt),
        out_shape=jax.ShapeDtypeStruct((m, d), F32),
        grid=(m // tt,),
        in_specs=[pl.BlockSpec((1, 1, TOP_K * tt), lambda i: (i, 0, 0), memory_space=pltpu.SMEM),
                  pl.BlockSpec((tt, d), lambda i: (i, 0)),
                  pl.BlockSpec((tt, LANES), lambda i: (i, 0)),
                  pl.BlockSpec(memory_space=pl.ANY)],
        out_specs=pl.BlockSpec((tt, d), lambda i: (i, 0)),
        scratch_shapes=[pltpu.VMEM((TOP_K, tt, d), F32), pltpu.SemaphoreType.DMA((TOP_K,))],
        compiler_params=_params(("arbitrary",)),
        name="moe_combine",
    )(pos3, x, wts, y)


def kernel(x_prompt, x_sample, state_conv, cache_kv, cache_win, page_table, norm_mix, norm_ffn, conv_w_in, conv_w, conv_w_out, ffn_w_gu, ffn_w_down, moe_router, moe_w_gu, moe_w_down, kv_norm, w_kv, k_norm, cmp_w1, cmp_w2, cmp_pe, w_qg, q_norm, w_o):
    batch, seq, d = x_prompt.shape
    dec_batch, dec_seq, _ = x_sample.shape
    n_pool, page_size = cache_kv.shape[:2]
    n_pages = page_table.shape[1]
    past_len = n_pages * page_size
    l_win = cache_win.shape[1]
    d_ff = ffn_w_down.shape[1]
    n_experts = moe_router.shape[2]
    n_heads = d // HEAD_DIM
    hpg = n_heads // N_KV
    sec_w = N_KV * HEAD_DIM
    m_prompt = batch * seq
    ms = dec_batch * dec_seq
    m = m_prompt + ms
    assert seq & (seq - 1) == 0 and dec_seq & (dec_seq - 1) == 0 and dec_seq >= CONV_W - 1
    assert m_prompt % ms == 0 and ms % 16 == 0 and seq % 128 == 0 and seq >= WINDOW + 128
    assert past_len % L_SLC == 0 and dec_seq <= L_SLC and page_size % L_SLC == 0
    assert norm_mix.shape[0] == 2 and l_win == WINDOW and past_len >= l_win

    x0 = jnp.concatenate([x_prompt.reshape(m_prompt, d), x_sample.reshape(ms, d)], axis=0)
    tm = _row_tile(m)

    (h0,) = _rmsnorm(x0, norm_mix[0:1])
    tn = _pick(d, (256, 128))
    tk = _pick(d, (1024, 512, 256, 128))
    b_gate, u = _matmul(
        h0, conv_w_in[0], col_offsets=(0, d, 2 * d), n_cols=d, tm=tm, tn=tn, tk=tk,
        epilogue=_epi_conv_in, out_shapes=[jax.ShapeDtypeStruct((m, d), F32)] * 2,
        out_specs=[_ij_spec(tm, tn)] * 2, name="conv_in")
    st = state_conv[0]
    zrow = jnp.zeros((dec_batch, dec_seq - 1, d), F32)
    s1 = jnp.concatenate([st[:, 1:2], zrow], axis=1).reshape(ms, d)
    s2 = jnp.concatenate([st[:, 0:1], st[:, 1:2], zrow[:, 1:]], axis=1).reshape(ms, d)
    z = _conv_gate(u, b_gate, conv_w[0], s1, s2, m_prompt=m_prompt, seq=seq, dec_seq=dec_seq)
    tn = _pick(d, (1024, 512, 256, 128))
    tk_r = _pick(d, (512, 256, 128))
    (x1,) = _matmul(
        z, conv_w_out[0], col_offsets=(0,), n_cols=d, tm=tm, tn=tn, tk=tk_r, epilogue=_epi_residual,
        extras=(x0,), extra_specs=(_ij_spec(tm, tn),), out_shapes=[jax.ShapeDtypeStruct((m, d), F32)],
        out_specs=[_ij_spec(tm, tn)], name="conv_out")
    u_p = u[:m_prompt].reshape(batch, seq, d)
    conv_prompt = u_p[:, seq - (CONV_W - 1):][None]
    conv_sample = u[m_prompt:].reshape(dec_batch, dec_seq, d)[:, dec_seq - (CONV_W - 1):][None]

    (h1,) = _rmsnorm(x1, norm_ffn[0:1])
    tnf = _pick(d_ff, (512, 256, 128))
    (act,) = _matmul(
        h1, ffn_w_gu[0], col_offsets=(0, d_ff), n_cols=d_ff, tm=tm, tn=tnf, tk=tk, epilogue=_epi_swiglu,
        out_shapes=[jax.ShapeDtypeStruct((m, d_ff), BF16)], out_specs=[_ij_spec(tm, tnf)], name="ffn_gu")
    tkf = _pick(d_ff, (512, 256, 128))
    (x2,) = _matmul(
        act, ffn_w_down[0], col_offsets=(0,), n_cols=d, tm=tm, tn=tn, tk=tkf, epilogue=_epi_residual,
        extras=(x1,), extra_specs=(_ij_spec(tm, tn),), out_shapes=[jax.ShapeDtypeStruct((m, d), F32)],
        out_specs=[_ij_spec(tm, tn)], name="ffn_down")

    hkv, h2 = _rmsnorm(x2, jnp.stack([kv_norm, norm_mix[1]]))
    ones = jnp.ones((HEAD_DIM,), F32)
    kv_gain = jnp.stack([ones, ones, k_norm[1], ones, k_norm[2], ones]).reshape(2 * N_BRANCH, 1, HEAD_DIM)
    kv_f32, kv_bf = _matmul(
        hkv, w_kv, col_offsets=(0,), n_cols=2 * N_BRANCH * sec_w, tm=tm, tn=sec_w, tk=tk, epilogue=_epi_kv,
        extras=(kv_gain,), extra_specs=(pl.BlockSpec((None, 1, HEAD_DIM), lambda i, j, k: (j, 0, 0)),),
        out_shapes=[jax.ShapeDtypeStruct((m, 2 * N_BRANCH * sec_w), F32),
                    jax.ShapeDtypeStruct((m, 2 * N_BRANCH * sec_w), BF16)],
        out_specs=[_ij_spec(tm, sec_w)] * 2, name="kv_proj")
    kv_p = kv_f32[:m_prompt].reshape(batch, seq, 2 * N_BRANCH, N_KV, HEAD_DIM)
    kv_s = kv_f32[m_prompt:].reshape(dec_batch, dec_seq, 2 * N_BRANCH, N_KV, HEAD_DIM)
    kv_prompt = kv_p[:, :, :4]
    win_prompt = kv_p[:, seq - min(WINDOW, seq):, 4:]
    kv_sample = kv_s[:, :, :4]
    win_sample = jnp.concatenate([cache_win, kv_s[:, :, 4:]], axis=1)[:, dec_seq:]

    wq = w_qg[0]
    (q,) = _matmul(
        h2, wq, col_offsets=(0,), n_cols=d, tm=tm, tn=tn, tk=tk_r, epilogue=_epi_q,
        extras=(q_norm[0:1],), extra_specs=(pl.BlockSpec((1, HEAD_DIM), lambda i, j, k: (0, 0)),),
        out_shapes=[jax.ShapeDtypeStruct((m, d), BF16)], out_specs=[_ij_spec(tm, tn)], name="q_proj")
    wg = wq[:, d:].reshape(d, N_KV, hpg, N_BRANCH).transpose(0, 1, 3, 2).reshape(d, N_KV, N_BRANCH * hpg)
    wg = jnp.pad(wg, ((0, 0), (0, 0), (0, LANES - N_BRANCH * hpg))).reshape(d, N_KV * LANES)
    (gates,) = _matmul(
        h2, wg, col_offsets=(0,), n_cols=N_KV * LANES, tm=tm, tn=N_KV * LANES, tk=tk, epilogue=_epi_gate,
        out_shapes=[jax.ShapeDtypeStruct((m, N_KV * LANES), F32)], out_specs=[_ij_spec(tm, N_KV * LANES)],
        name="gate_proj")

    w1r = cmp_w1.reshape(2, R_CMP, STRIDE * HEAD_DIM, HEAD_DIM)
    per = cmp_pe.reshape(2, R_CMP, 1, STRIDE * HEAD_DIM)
    kn0 = k_norm[0:1]
    kc_p = _compress_prompt(kv_f32, w1r, per, cmp_w2, kn0, batch=batch, seq=seq)
    pt_flat = page_table.reshape(-1)
    kc_s = _compress_sample(cache_kv, pt_flat, w1r, per, cmp_w2, kn0, dec_batch=dec_batch, n_pages=n_pages,
                            page_size=page_size)

    o = _nsa_prompt(q, gates, kc_p, kv_bf, m_total=m, batch=batch, seq=seq, hpg=hpg)
    win2d = cache_win.reshape(dec_batch * l_win, 2 * sec_w)
    o = _nsa_sample(q, gates, kc_s, kv_f32, cache_kv, win2d, pt_flat, o, m_prompt=m_prompt,
                    dec_batch=dec_batch, dec_seq=dec_seq, hpg=hpg, past_len=past_len, n_pages=n_pages,
                    page_size=page_size, l_win=l_win)
    (x3,) = _matmul(
        o, w_o[0], col_offsets=(0,), n_cols=d, tm=tm, tn=tn, tk=tk_r, epilogue=_epi_residual,
        extras=(x2,), extra_specs=(_ij_spec(tm, tn),), out_shapes=[jax.ShapeDtypeStruct((m, d), F32)],
        out_specs=[_ij_spec(tm, tn)], name="attn_out")

    router_p = jnp.pad(moe_router[0], ((0, 0), (0, LANES - n_experts)))
    hp, ridx, wts = _router(x3, norm_ffn[1:2], router_p, n_experts)
    wgu = moe_w_gu[0].reshape(n_experts * d, 2 * d_ff)
    wdn = moe_w_down[0].reshape(n_experts * d_ff, d)
    sub = MOE_SUB_ROWS
    rc = -(-(TOP_K * m * 12 // (10 * n_experts)) // (2 * sub)) * (2 * sub)
    nch = TOP_K * m // rc + n_experts
    pos, nsub, cx, ce = _moe_plan(ridx[:, :TOP_K], n_experts, rc, nch, sub)
    xs = _moe_scatter(hp, pos, nch * rc)
    y = _moe_experts(xs, wgu, wdn, nsub, cx, ce, d=d, d_ff=d_ff, rc=rc, nch=nch, sub=sub)
    x4 = _moe_combine(x3, wts, y, pos)

    y_prompt = x4[:m_prompt].reshape(batch, seq, d)
    y_sample = x4[m_prompt:].reshape(dec_batch, dec_seq, d)
    return (y_prompt, y_sample, conv_prompt, kv_prompt, win_prompt, conv_sample, kv_sample, win_sample)
```
